```python
import math
import jax
import jax.numpy as jnp
from jax import lax
import numpy as np

D_MODEL = 1024
BATCH = 4
SEQ = 4096
DEPTH = 4
DEC_BATCH = 8
DEC_SEQ = 64
PAST_LEN = 2048

CHUNK = 64
LRU_HEADS = 8
LRU_HEAD_DIM = 64
LRU_WIDTH = LRU_HEADS * LRU_HEAD_DIM
CONV_W = 4
LRU_C = 8.0
MLA_HEADS = 8
NOPE_DIM = 64
ROPE_DIM = 32
V_DIM = 64
Q_RANK = 384
KV_RANK = 256
ROPE_THETA = 10000.0
Q_BLOCK = 128
IN_COLS = 2 * LRU_WIDTH + Q_RANK + KV_RANK + ROPE_DIM
MIX_WIDTH = LRU_WIDTH + MLA_HEADS * V_DIM
N_EXPERTS = 32
TOP_K = 4
D_FF = 1024
SWIGLU_LIMIT = 7.0
SWIGLU_ALPHA = 1.702
MOE_BLOCK = 128
DN_ALPHA = (2.0 * DEPTH) ** 0.25
DN_BETA = (8.0 * DEPTH) ** -0.25
LN_EPS = 1e-5
RMS_EPS = 1e-6

kernel_name = 'hymba_rglru_mla_moe_deepnorm_stream_step'

F32 = jnp.float32


def layer_norm(x, g, b):
    xf = x.astype(F32)
    mu = jnp.mean(xf, axis=-1, keepdims=True)
    xc = xf - mu
    var = jnp.mean(xc * xc, axis=-1, keepdims=True)
    y = xc * lax.rsqrt(var + LN_EPS) * g.astype(F32) + b.astype(F32)
    return y.astype(x.dtype)


def rms_norm(x, g):
    xf = x.astype(F32)
    y = xf * lax.rsqrt(jnp.mean(xf * xf, axis=-1, keepdims=True) + RMS_EPS) * g.astype(F32)
    return y.astype(x.dtype)


def rope(x, pos):
    half = ROPE_DIM // 2
    freqs = ROPE_THETA ** (-2.0 * jnp.arange(half, dtype=F32) / ROPE_DIM)
    ang = pos[:, None] * freqs[None, :]
    cos = jnp.cos(ang)[None, :, None, :]
    sin = jnp.sin(ang)[None, :, None, :]
    xf = x.astype(F32)
    x1, x2 = xf[..., :half], xf[..., half:]
    out = jnp.concatenate([x1 * cos - x2 * sin, x2 * cos + x1 * sin], axis=-1)
    return out.astype(x.dtype)


def causal_conv(xb, conv_state, w, b):
    S = xb.shape[1]
    xpad = jnp.concatenate([conv_state.astype(xb.dtype), xb], axis=1)
    y = b + sum(xpad[:, k:k + S] * w[k] for k in range(CONV_W))
    return y, xpad[:, -(CONV_W - 1):]


def rg_lru(xc, h0, w_rg, b_rg, w_ig, b_ig, lam):
    B, S, _ = xc.shape
    xh = xc.reshape(B, S, LRU_HEADS, LRU_HEAD_DIM)
    r = jax.nn.sigmoid(jnp.einsum('bshi,hij->bshj', xh, w_rg).reshape(B, S, LRU_WIDTH) + b_rg)
    ig = jax.nn.sigmoid(jnp.einsum('bshi,hij->bshj', xh, w_ig).reshape(B, S, LRU_WIDTH) + b_ig)
    log_a = -LRU_C * r.astype(F32) * jax.nn.softplus(-lam.astype(F32))
    a = jnp.exp(log_a)
    mult = jnp.sqrt(-jnp.expm1(2.0 * log_a))
    bt = mult * ig.astype(F32) * xc.astype(F32)

    def step(h, ab):
        a_t, b_t = ab
        h = a_t * h + b_t
        return h, h

    h_last, hs = lax.scan(step, h0.astype(F32), (jnp.swapaxes(a, 0, 1), jnp.swapaxes(bt, 0, 1)))
    return jnp.swapaxes(hs, 0, 1).astype(xc.dtype), h_last.astype(xc.dtype)


def mla_attention(q_lat, q_rope, ckv, krope, q_pos, k_pos):
    B, S, H, R = q_lat.shape
    blk = S if S <= Q_BLOCK else Q_BLOCK
    nb = S // blk
    scale = (NOPE_DIM + ROPE_DIM) ** -0.5
    ckv32 = ckv.astype(F32)
    kr32 = krope.astype(F32)

    def one_block(args):
        ql, qr, qp = args
        s = (jnp.einsum('bqhr,btr->bhqt', ql.astype(F32), ckv32)
             + jnp.einsum('bqhd,btd->bhqt', qr.astype(F32), kr32)) * scale
        visible = k_pos[None, :] < ((qp // CHUNK) + 1)[:, None] * CHUNK
        s = jnp.where(visible[None, None], s, -jnp.inf)
        p = jax.nn.softmax(s, axis=-1)
        return jnp.einsum('bhqt,btr->bqhr', p, ckv32)

    blocks = (jnp.swapaxes(q_lat.reshape(B, nb, blk, H, R), 0, 1),
              jnp.swapaxes(q_rope.reshape(B, nb, blk, H, ROPE_DIM), 0, 1),
              q_pos.reshape(nb, blk))
    o = lax.map(one_block, blocks)
    return jnp.swapaxes(o, 0, 1).reshape(B, S, H, R)


def token_mixer(x, past_ckv, past_krope, conv_state, lru_state, pos0,
                w_in, conv_w, conv_b, w_rg, b_rg, w_ig, b_ig, lru_lambda,
                q_norm_g, w_uq, kv_norm_g, w_uk, w_uv, w_o):
    B, S, _ = x.shape
    u = x @ w_in
    o1 = LRU_WIDTH
    o2 = 2 * LRU_WIDTH
    o3 = o2 + Q_RANK
    o4 = o3 + KV_RANK
    ux, ug, uq, ukv, ukr = u[..., :o1], u[..., o1:o2], u[..., o2:o3], u[..., o3:o4], u[..., o4:]
    xc, new_conv = causal_conv(ux, conv_state, conv_w, conv_b)
    hs, new_h = rg_lru(xc, lru_state, w_rg, b_rg, w_ig, b_ig, lru_lambda)
    lru_out = jax.nn.gelu(ug) * hs
    pos = pos0 + jnp.arange(S, dtype=jnp.int32)
    posf = pos.astype(F32)
    q = (rms_norm(uq, q_norm_g) @ w_uq).reshape(B, S, MLA_HEADS, NOPE_DIM + ROPE_DIM)
    q_nope = q[..., :NOPE_DIM]
    q_rope = rope(q[..., NOPE_DIM:], posf)
    c_new = rms_norm(ukv, kv_norm_g)
    kr_new = rope(ukr[:, :, None, :], posf)[:, :, 0, :]
    ckv = jnp.concatenate([past_ckv.astype(x.dtype), c_new], axis=1)
    krope = jnp.concatenate([past_krope.astype(x.dtype), kr_new], axis=1)
    k_pos = jnp.arange(ckv.shape[1], dtype=jnp.int32)
    q_lat = jnp.einsum('bshn,rhn->bshr', q_nope, w_uk)
    o_lat = mla_attention(q_lat, q_rope, ckv, krope, pos, k_pos)
    attn = jnp.einsum('bshr,rhv->bshv', o_lat.astype(x.dtype), w_uv).reshape(B, S, MLA_HEADS * V_DIM)
    mix = jnp.concatenate([lru_out, attn], axis=-1) @ w_o
    return mix, c_new, kr_new, new_conv, new_h


def moe(x, w_router, b_router, w_gu, b_gu, w_down, b_down):
    B, S, D = x.shape
    N = B * S
    xf = x.reshape(N, D)
    logits = xf.astype(F32) @ w_router.astype(F32) + b_router.astype(F32)
    top_vals, top_idx = lax.top_k(logits, TOP_K)
    gates = jax.nn.softmax(top_vals, axis=-1).astype(x.dtype)
    NK = N * TOP_K
    flat_e = top_idx.reshape(NK)
    order = jnp.argsort(flat_e)
    e_sorted = flat_e[order]
    tok_sorted = order // TOP_K
    g_sorted = gates.reshape(NK)[order]
    counts = jnp.bincount(flat_e, length=N_EXPERTS)
    padded = (counts + MOE_BLOCK - 1) // MOE_BLOCK * MOE_BLOCK
    pad_end = jnp.cumsum(padded)
    pad_start = pad_end - padded
    cnt_start = jnp.cumsum(counts) - counts
    dest = pad_start[e_sorted] + jnp.arange(NK, dtype=jnp.int32) - cnt_start[e_sorted]
    n_blocks = -(-NK // MOE_BLOCK) + N_EXPERTS
    rows = jnp.full((n_blocks * MOE_BLOCK,), N, jnp.int32).at[dest].set(tok_sorted)
    x_pad = jnp.concatenate([xf, jnp.zeros((1, D), xf.dtype)], axis=0)
    xb = x_pad[rows].reshape(n_blocks, MOE_BLOCK, D)
    block_e = jnp.minimum(
        jnp.searchsorted(pad_end, jnp.arange(n_blocks, dtype=jnp.int32) * MOE_BLOCK, side='right'),
        N_EXPERTS - 1)

    def expert_block(args):
        xr, e = args
        h = xr @ w_gu[e] + b_gu[e]
        hg, hl = h[:, :D_FF], h[:, D_FF:]
        hg = jnp.minimum(hg, SWIGLU_LIMIT)
        hl = jnp.clip(hl, -SWIGLU_LIMIT, SWIGLU_LIMIT)
        act = hg * jax.nn.sigmoid(SWIGLU_ALPHA * hg) * (hl + 1.0)
        return act @ w_down[e] + b_down[e]

    yb = lax.map(expert_block, (xb, block_e)).reshape(n_blocks * MOE_BLOCK, D)
    y = jax.ops.segment_sum(yb[dest] * g_sorted[:, None], tok_sorted, num_segments=N)
    return y.reshape(B, S, D).astype(x.dtype)


def trunk(x, past_ckv, past_krope, conv_state, lru_state, pos0,
          w_in, conv_w, conv_b, w_rg, b_rg, w_ig, b_ig, lru_lambda,
          q_norm_g, w_uq, kv_norm_g, w_uk, w_uv, w_o, ln1_g, ln1_b,
          w_router, b_router, w_gu, b_gu, w_down, b_down, ln2_g, ln2_b):
    ckvs, krs, convs, hs = [], [], [], []
    for l in range(DEPTH):
        mix, c_new, kr_new, new_conv, new_h = token_mixer(
            x, past_ckv[l], past_krope[l], conv_state[l], lru_state[l], pos0,
            w_in[l], conv_w[l], conv_b[l], w_rg[l], b_rg[l], w_ig[l], b_ig[l], lru_lambda[l],
            q_norm_g[l], w_uq[l], kv_norm_g[l], w_uk[l], w_uv[l], w_o[l])
        x = layer_norm(DN_ALPHA * x + mix, ln1_g[l], ln1_b[l])
        x = layer_norm(DN_ALPHA * x + moe(x, w_router[l], b_router[l], w_gu[l], b_gu[l], w_down[l], b_down[l]),
                       ln2_g[l], ln2_b[l])
        ckvs.append(c_new)
        krs.append(kr_new)
        convs.append(new_conv)
        hs.append(new_h)
    return x, jnp.stack(ckvs), jnp.stack(krs), jnp.stack(convs), jnp.stack(hs)


def setup_inputs(seed: int = 0) -> dict:
    key = jax.random.key(seed)
    ks = jax.random.split(key, 40)

    def nrm(k, shape, scale):
        return jax.random.normal(k, shape, F32) * scale

    a0 = jax.random.uniform(ks[10], (DEPTH, LRU_WIDTH), F32, 0.9, 0.999)
    return {
        'x_prompt': nrm(ks[0], (BATCH, SEQ, D_MODEL), 1.0),
        'x_sample': nrm(ks[1], (DEC_BATCH, DEC_SEQ, D_MODEL), 1.0),
        'cache_ckv': nrm(ks[2], (DEPTH, DEC_BATCH, PAST_LEN, KV_RANK), 1.0),
        'cache_krope': nrm(ks[3], (DEPTH, DEC_BATCH, PAST_LEN, ROPE_DIM), 1.0),
        'state_conv': nrm(ks[4], (DEPTH, DEC_BATCH, CONV_W - 1, LRU_WIDTH), 1.0),
        'state_lru': nrm(ks[5], (DEPTH, DEC_BATCH, LRU_WIDTH), 0.5),
        'w_in': nrm(ks[6], (DEPTH, D_MODEL, IN_COLS), D_MODEL ** -0.5),
        'conv_w': nrm(ks[7], (DEPTH, CONV_W, LRU_WIDTH), CONV_W ** -0.5),
        'conv_b': nrm(ks[8], (DEPTH, LRU_WIDTH), 0.02),
        'w_rg': nrm(ks[9], (DEPTH, LRU_HEADS, LRU_HEAD_DIM, LRU_HEAD_DIM), LRU_HEAD_DIM ** -0.5),
        'b_rg': nrm(ks[11], (DEPTH, LRU_WIDTH), 0.02),
        'w_ig': nrm(ks[12], (DEPTH, LRU_HEADS, LRU_HEAD_DIM, LRU_HEAD_DIM), LRU_HEAD_DIM ** -0.5),
        'b_ig': nrm(ks[13], (DEPTH, LRU_WIDTH), 0.02),
        'lru_lambda': jnp.log(a0) - jnp.log1p(-a0),
        'q_norm_g': 1.0 + nrm(ks[14], (DEPTH, Q_RANK), 0.02),
        'w_uq': nrm(ks[15], (DEPTH, Q_RANK, MLA_HEADS * (NOPE_DIM + ROPE_DIM)), Q_RANK ** -0.5),
        'kv_norm_g': 1.0 + nrm(ks[16], (DEPTH, KV_RANK), 0.02),
        'w_uk': nrm(ks[17], (DEPTH, KV_RANK, MLA_HEADS, NOPE_DIM), KV_RANK ** -0.5),
        'w_uv': nrm(ks[18], (DEPTH, KV_RANK, MLA_HEADS, V_DIM), DN_BETA * KV_RANK ** -0.5),
        'w_o': nrm(ks[19], (DEPTH, MIX_WIDTH, D_MODEL), DN_BETA * MIX_WIDTH ** -0.5),
        'ln1_g': 1.0 + nrm(ks[20], (DEPTH, D_MODEL), 0.02),
        'ln1_b': nrm(ks[21], (DEPTH, D_MODEL), 0.02),
        'w_router': nrm(ks[22], (DEPTH, D_MODEL, N_EXPERTS), D_MODEL ** -0.5),
        'b_router': nrm(ks[23], (DEPTH, N_EXPERTS), 0.01),
        'w_gu': nrm(ks[24], (DEPTH, N_EXPERTS, D_MODEL, 2 * D_FF), D_MODEL ** -0.5),
        'b_gu': nrm(ks[25], (DEPTH, N_EXPERTS, 2 * D_FF), 0.02),
        'w_down': nrm(ks[26], (DEPTH, N_EXPERTS, D_FF, D_MODEL), DN_BETA * D_FF ** -0.5),
        'b_down': nrm(ks[27], (DEPTH, N_EXPERTS, D_MODEL), 0.02),
        'ln2_g': 1.0 + nrm(ks[28], (DEPTH, D_MODEL), 0.02),
        'ln2_b': nrm(ks[29], (DEPTH, D_MODEL), 0.02),
    }


def reference(x_prompt, x_sample, cache_ckv, cache_krope, state_conv, state_lru,
              w_in, conv_w, conv_b, w_rg, b_rg, w_ig, b_ig, lru_lambda,
              q_norm_g, w_uq, kv_norm_g, w_uk, w_uv, w_o, ln1_g, ln1_b,
              w_router, b_router, w_gu, b_gu, w_down, b_down, ln2_g, ln2_b):
    B = x_prompt.shape[0]
    dt = x_prompt.dtype
    y_prompt, ckv_p, kr_p, conv_p, h_p = trunk(
        x_prompt,
        jnp.zeros((DEPTH, B, 0, KV_RANK), dt), jnp.zeros((DEPTH, B, 0, ROPE_DIM), dt),
        jnp.zeros((DEPTH, B, CONV_W - 1, LRU_WIDTH), dt), jnp.zeros((DEPTH, B, LRU_WIDTH), dt), 0,
        w_in, conv_w, conv_b, w_rg, b_rg, w_ig, b_ig, lru_lambda,
        q_norm_g, w_uq, kv_norm_g, w_uk, w_uv, w_o, ln1_g, ln1_b,
        w_router, b_router, w_gu, b_gu, w_down, b_down, ln2_g, ln2_b)
    past_len = cache_ckv.shape[2]
    y_sample, ckv_s, kr_s, conv_s, h_s = trunk(
        x_sample, cache_ckv, cache_krope, state_conv, state_lru, past_len,
        w_in, conv_w, conv_b, w_rg, b_rg, w_ig, b_ig, lru_lambda,
        q_norm_g, w_uq, kv_norm_g, w_uk, w_uv, w_o, ln1_g, ln1_b,
        w_router, b_router, w_gu, b_gu, w_down, b_down, ln2_g, ln2_b)
    return (y_prompt, y_sample, ckv_p, kr_p, conv_p, h_p, ckv_s, kr_s, conv_s, h_s)
```

```python
import functools
from typing import NamedTuple

import jax
import jax.numpy as jnp
from jax import lax
from jax.experimental import pallas as pl
from jax.experimental.pallas import tpu as pltpu

F32 = jnp.float32
BF16 = jnp.bfloat16
I32 = jnp.int32

D_MODEL = 1024
CHUNK = 64
LRU_WIDTH = 512
CONV_W = 4
LRU_C = 8.0
MLA_HEADS = 8
NOPE_DIM = 64
ROPE_DIM = 32
V_DIM = 64
Q_RANK = 384
KV_RANK = 256
ROPE_THETA = 10000.0
N_EXPERTS = 32
TOP_K = 4
D_FF = 1024
SWIGLU_LIMIT = 7.0
SWIGLU_ALPHA = 1.702
LN_EPS = 1e-5
RMS_EPS = 1e-6

V7X_SUBLANES = 8
V7X_LANES = 128
V7X_MXU_DIM = 256
V7X_VMEM_BYTES = 64 * 1024 * 1024

TM = 512
TQ = 512
RT = 256
EBLK = 256
CH = V7X_SUBLANES
SLOTS = 1280
IN_COLS_EXT = 2 * LRU_WIDTH + Q_RANK + KV_RANK + 2 * ROPE_DIM
Q_COLS_EXT = MLA_HEADS * (NOPE_DIM + 2 * ROPE_DIM)

assert SLOTS >= RT * TOP_K + N_EXPERTS * (CH - 1) and SLOTS % CH == 0


class Dims(NamedTuple):
    depth: int
    batch: int
    seq: int
    dec_batch: int
    dec_seq: int
    past: int

    @property
    def n_prompt(self):
        return self.batch * self.seq

    @property
    def n_sample(self):
        return self.dec_batch * self.dec_seq

    @property
    def n_tok(self):
        return self.n_prompt + self.n_sample

    @property
    def n_rt(self):
        return self.n_tok // RT

    @property
    def n_eblk(self):
        chunks = self.n_tok * TOP_K // CH + self.n_rt * N_EXPERTS + N_EXPERTS * (EBLK // CH - 1)
        return -(-chunks // (EBLK // CH))


def _vmem(nbytes):
    return pltpu.CompilerParams(vmem_limit_bytes=min(nbytes, V7X_VMEM_BYTES - 8 * 1024 * 1024))


def _cparams(sem, nbytes):
    return pltpu.CompilerParams(dimension_semantics=sem,
                                vmem_limit_bytes=min(nbytes, V7X_VMEM_BYTES - 8 * 1024 * 1024))


MIB = 1024 * 1024


def _dot(a, b):
    return jnp.dot(a, b, preferred_element_type=F32)


def _dot_nt(a, b):
    return lax.dot_general(a, b, (((1,), (1,)), ((), ())), preferred_element_type=F32)


def _layer_norm(v, g, b):
    mu = jnp.mean(v, axis=-1, keepdims=True)
    vc = v - mu
    var = jnp.mean(vc * vc, axis=-1, keepdims=True)
    return vc * lax.rsqrt(var + LN_EPS) * g + b


def _rms_norm(v, g):
    return v * lax.rsqrt(jnp.mean(v * v, axis=-1, keepdims=True) + RMS_EPS) * g


def _inproj_kernel(x_ref, win_ref, tabq_ref, tabk_ref, qg_ref, kvg_ref, wuq_ref, wuk_ref,
                   ux_ref, ug_ref, c_ref, cb_ref, kr_ref, krb_ref, qlat_ref, qrope_ref, *, scale):
    u = _dot(x_ref[...].astype(BF16), win_ref[...])
    o1, o2 = LRU_WIDTH, 2 * LRU_WIDTH
    o3 = o2 + Q_RANK
    o4 = o3 + KV_RANK
    ux_ref[...] = u[:, :o1]
    ug_ref[...] = u[:, o1:o2]
    c_new = _rms_norm(u[:, o3:o4], kvg_ref[...])
    c_ref[...] = c_new
    cb_ref[...] = c_new.astype(BF16)
    tk = tabk_ref[...]
    kr = (u[:, o4:o4 + ROPE_DIM] * tk[:, :ROPE_DIM]
          + u[:, o4 + ROPE_DIM:o4 + 2 * ROPE_DIM] * tk[:, ROPE_DIM:])
    kr_ref[...] = kr
    krb_ref[...] = kr.astype(BF16)
    qn = _rms_norm(u[:, o2:o3], qg_ref[...]).astype(BF16)
    q = _dot(qn, wuq_ref[...])
    n_nope = MLA_HEADS * NOPE_DIM
    n_rope = MLA_HEADS * ROPE_DIM
    tq = tabq_ref[...]
    q_rope = q[:, n_nope:n_nope + n_rope] * tq[:, :n_rope] + q[:, n_nope + n_rope:] * tq[:, n_rope:]
    qrope_ref[...] = (q_rope * scale).astype(BF16)
    qnb = q[:, :n_nope].astype(BF16)
    half = n_nope // 2
    hw = (MLA_HEADS // 2) * KV_RANK
    qlat_ref[:, :hw] = (_dot(qnb[:, :half], wuk_ref[0]) * scale).astype(BF16)
    qlat_ref[:, hw:] = (_dot(qnb[:, half:], wuk_ref[1]) * scale).astype(BF16)


def _inproj(x, win, tabq, tabk, qg, kvg, wuq, wuk, dims):
    n = dims.n_tok
    n_pt = dims.n_prompt // TM
    tps = dims.seq // TM
    tab_map = lambda i: (jnp.where(i < n_pt, i % tps, tps), 0)
    row = lambda w: pl.BlockSpec((TM, w), lambda i: (i, 0))
    full = lambda a: pl.BlockSpec(a.shape, lambda i: (0,) * a.ndim)
    scale = (NOPE_DIM + ROPE_DIM) ** -0.5
    return pl.pallas_call(
        functools.partial(_inproj_kernel, scale=scale),
        grid=(n // TM,),
        in_specs=[row(D_MODEL), full(win),
                  pl.BlockSpec((TM, tabq.shape[1]), tab_map), pl.BlockSpec((TM, tabk.shape[1]), tab_map),
                  full(qg), full(kvg), full(wuq), full(wuk)],
        out_specs=[row(LRU_WIDTH), row(LRU_WIDTH), row(KV_RANK), row(KV_RANK), row(ROPE_DIM), row(ROPE_DIM),
                   row(MLA_HEADS * KV_RANK), row(MLA_HEADS * ROPE_DIM)],
        out_shape=[jax.ShapeDtypeStruct((n, LRU_WIDTH), F32), jax.ShapeDtypeStruct((n, LRU_WIDTH), F32),
                   jax.ShapeDtypeStruct((n, KV_RANK), F32), jax.ShapeDtypeStruct((n, KV_RANK), BF16),
                   jax.ShapeDtypeStruct((n, ROPE_DIM), F32), jax.ShapeDtypeStruct((n, ROPE_DIM), BF16),
                   jax.ShapeDtypeStruct((n, MLA_HEADS * KV_RANK), BF16),
                   jax.ShapeDtypeStruct((n, MLA_HEADS * ROPE_DIM), BF16)],
        compiler_params=_cparams(("parallel",), 48 * MIB),
        name="inproj",
    )(x, win, tabq, tabk, qg, kvg, wuq, wuk)


def _linear_scan(a, b):
    n = a.shape[0]
    row = lax.broadcasted_iota(I32, a.shape, 0)
    d = 1
    while d < n:
        keep = row >= d
        a_sh = jnp.where(keep, pltpu.roll(a, d, 0), 1.0)
        b_sh = jnp.where(keep, pltpu.roll(b, d, 0), 0.0)
        b = b + a * b_sh
        a = a * a_sh
        d *= 2
    return a, b


def _gelu_tanh(v):
    return 0.5 * v * (1.0 + jnp.tanh(0.7978845608028654 * (v + 0.044715 * v * v * v)))


def _lru_kernel(ux_ref, ug_ref, st0_ref, cw_ref, cb_ref, wrg_ref, brg_ref, wig_ref, big_ref,
                lam_ref, out_ref, st_ref, xs_sc, carry_sc, *, seg, tps):
    t = pl.program_id(0) % tps
    lo = V7X_SUBLANES - (CONV_W - 1)

    @pl.when(t == 0)
    def _():
        carry_sc[...] = st0_ref[0]

    st_ref[0] = jnp.zeros(st_ref.shape[1:], F32)

    xs_sc[lo:V7X_SUBLANES, :] = carry_sc[lo:V7X_SUBLANES, :]
    xs_sc[V7X_SUBLANES:V7X_SUBLANES + seg, :] = ux_ref[...]
    cw = cw_ref[...]
    xc = cb_ref[...] + xs_sc[lo:lo + seg, :] * cw[0:1, :]
    for k in range(1, CONV_W):
        xc = xc + xs_sc[lo + k:lo + k + seg, :] * cw[k:k + 1, :]
    new_conv = xs_sc[seg + lo:seg + V7X_SUBLANES, :]
    carry_sc[lo:V7X_SUBLANES, :] = new_conv
    st_ref[0, lo:V7X_SUBLANES, :] = new_conv

    xcb = xc.astype(BF16)
    half = LRU_WIDTH // 2
    for g in range(2):
        cs = slice(g * half, (g + 1) * half)
        xg = xc[:, cs]
        r = jax.nn.sigmoid(_dot(xcb[:, cs], wrg_ref[g]) + brg_ref[:, cs])
        ig = jax.nn.sigmoid(_dot(xcb[:, cs], wig_ref[g]) + big_ref[:, cs])
        nl = -lam_ref[:, cs]
        softplus = jnp.maximum(nl, 0.0) + jnp.log1p(jnp.exp(-jnp.abs(nl)))
        log_a = -LRU_C * r * softplus
        a = jnp.exp(log_a)
        th = jnp.tanh(log_a)
        bt = jnp.sqrt(-2.0 * th / (1.0 - th)) * ig * xg
        a_cum, h = _linear_scan(a, bt)
        h = h + a_cum * carry_sc[0:1, cs]
        h_last = h[seg - 1:seg, :]
        carry_sc[0:1, cs] = h_last
        st_ref[0, 0:1, cs] = h_last
        out_ref[:, cs] = (_gelu_tanh(ug_ref[:, cs]) * h).astype(BF16)


def _lru_state(conv, h):
    lo = V7X_SUBLANES - (CONV_W - 1)
    return jnp.concatenate([h[:, None, :], jnp.zeros((h.shape[0], lo - 1, h.shape[1]), F32), conv], axis=1)


def _lru(ux, ug, st0, lw, prev_out, *, seg, n_seg, tps, row0):
    n = ux.shape[0]
    blk0 = row0 // seg
    row = pl.BlockSpec((seg, LRU_WIDTH), lambda i: (blk0 + i, 0))
    full = lambda a: pl.BlockSpec(a.shape, lambda i: (0,) * a.ndim)
    n_seq = n_seg // tps
    weights = [lw["conv_w"], lw["conv_b"], lw["wrg"], lw["b_rg"], lw["wig"], lw["b_ig"], lw["lam"]]
    in_specs = [row, row,
                pl.BlockSpec((1, V7X_SUBLANES, LRU_WIDTH), lambda i: (i // tps, 0, 0))] + [full(w) for w in weights]
    args = [ux, ug, st0] + weights
    aliases = {}
    if prev_out is not None:
        in_specs.append(pl.BlockSpec(memory_space=pl.ANY))
        args.append(prev_out)
        aliases = {len(args) - 1: 0}

    def body(*refs):
        refs = list(refs)
        if prev_out is not None:
            del refs[len(args) - 1]
        _lru_kernel(*refs, seg=seg, tps=tps)

    return pl.pallas_call(
        body,
        grid=(n_seg,),
        in_specs=in_specs,
        out_specs=[row, pl.BlockSpec((1, V7X_SUBLANES, LRU_WIDTH), lambda i: (i // tps, 0, 0))],
        out_shape=[jax.ShapeDtypeStruct((n, LRU_WIDTH), BF16),
                   jax.ShapeDtypeStruct((n_seq, V7X_SUBLANES, LRU_WIDTH), F32)],
        scratch_shapes=[pltpu.VMEM((seg + V7X_SUBLANES, LRU_WIDTH), F32),
                        pltpu.VMEM((V7X_SUBLANES, LRU_WIDTH), F32)],
        input_output_aliases=aliases,
        compiler_params=_cparams(("arbitrary",), 40 * MIB),
        name="lru_seg%d" % seg,
    )(*args)


def _attn_prompt_kernel(qlat_ref, qrope_ref, k_ref, kr_ref, wuv_ref, o_ref, m_sc, l_sc, acc_sc):
    i = pl.program_id(1)
    j = pl.program_id(2)

    @pl.when(j == 0)
    def _():
        m_sc[...] = jnp.full(m_sc.shape, -jnp.inf, F32)
        l_sc[...] = jnp.zeros(l_sc.shape, F32)
        acc_sc[...] = jnp.zeros(acc_sc.shape, F32)

    @pl.when(j <= i)
    def _():
        k = k_ref[...]
        kr = kr_ref[...]
        qrow = lax.broadcasted_iota(I32, (TQ, TQ), 0)
        kcol = lax.broadcasted_iota(I32, (TQ, TQ), 1)
        visible = (kcol < (qrow // CHUNK + 1) * CHUNK) | (j < i)
        for h in range(MLA_HEADS):
            s = (_dot_nt(qlat_ref[:, h * KV_RANK:(h + 1) * KV_RANK], k)
                 + _dot_nt(qrope_ref[:, h * ROPE_DIM:(h + 1) * ROPE_DIM], kr))
            s = jnp.where(visible, s, -jnp.inf)
            m_prev = m_sc[h]
            m_new = jnp.maximum(m_prev, jnp.max(s, axis=1, keepdims=True))
            alpha = jnp.exp(m_prev - m_new)
            p = jnp.exp(s - m_new)
            l_sc[h] = alpha * l_sc[h] + jnp.sum(p, axis=1, keepdims=True)
            acc_sc[h] = alpha * acc_sc[h] + _dot(p.astype(BF16), k)
            m_sc[h] = m_new

    @pl.when(j == i)
    def _():
        for h in range(MLA_HEADS):
            o_lat = (acc_sc[h] / l_sc[h]).astype(BF16)
            o_ref[:, h * V_DIM:(h + 1) * V_DIM] = _dot(o_lat, wuv_ref[h]).astype(BF16)


def _attn_prompt(qlat, qrope, cb, krb, wuv, dims):
    n = dims.n_tok
    tps = dims.seq // TQ
    qmap = lambda b, i, j: (b * tps + i, 0)
    kmap = lambda b, i, j: (b * tps + jnp.minimum(i, j), 0)
    return pl.pallas_call(
        _attn_prompt_kernel,
        grid=(dims.batch, tps, tps),
        in_specs=[pl.BlockSpec((TQ, MLA_HEADS * KV_RANK), qmap), pl.BlockSpec((TQ, MLA_HEADS * ROPE_DIM), qmap),
                  pl.BlockSpec((TQ, KV_RANK), kmap), pl.BlockSpec((TQ, ROPE_DIM), kmap),
                  pl.BlockSpec(wuv.shape, lambda b, i, j: (0, 0, 0))],
        out_specs=pl.BlockSpec((TQ, MLA_HEADS * V_DIM), qmap),
        out_shape=jax.ShapeDtypeStruct((n, MLA_HEADS * V_DIM), BF16),
        scratch_shapes=[pltpu.VMEM((MLA_HEADS, TQ, 1), F32), pltpu.VMEM((MLA_HEADS, TQ, 1), F32),
                        pltpu.VMEM((MLA_HEADS, TQ, KV_RANK), F32)],
        compiler_params=_cparams(("parallel", "parallel", "arbitrary"), 48 * MIB),
        name="attn_prompt",
    )(qlat, qrope, cb, krb, wuv)


def _attn_sample_kernel(qlat_ref, qrope_ref, pk_ref, pkr_ref, k_ref, kr_ref, wuv_ref, prev_ref, o_ref, *, dec_seq):
    del prev_ref
    q = jnp.concatenate([qlat_ref[:, h * KV_RANK:(h + 1) * KV_RANK] for h in range(MLA_HEADS)], axis=0)
    qr = jnp.concatenate([qrope_ref[:, h * ROPE_DIM:(h + 1) * ROPE_DIM] for h in range(MLA_HEADS)], axis=0)
    pk = pk_ref[0, 0].astype(BF16)
    pkr = pkr_ref[0, 0].astype(BF16)
    k = k_ref[...]
    kr = kr_ref[...]
    s_past = _dot_nt(q, pk) + _dot_nt(qr, pkr)
    s_new = _dot_nt(q, k) + _dot_nt(qr, kr)
    m = jnp.maximum(jnp.max(s_past, axis=1, keepdims=True), jnp.max(s_new, axis=1, keepdims=True))
    p_past = jnp.exp(s_past - m)
    p_new = jnp.exp(s_new - m)
    denom = jnp.sum(p_past, axis=1, keepdims=True) + jnp.sum(p_new, axis=1, keepdims=True)
    o_lat = (_dot(p_past.astype(BF16), pk) + _dot(p_new.astype(BF16), k)) / denom
    for h in range(MLA_HEADS):
        o_h = o_lat[h * dec_seq:(h + 1) * dec_seq, :].astype(BF16)
        o_ref[:, h * V_DIM:(h + 1) * V_DIM] = _dot(o_h, wuv_ref[h]).astype(BF16)


def _attn_sample(qlat, qrope, cache_ckv, cache_krope, cb, krb, wuv, attn_prev, layer, dims):
    ds = dims.dec_seq
    blk0 = dims.n_prompt // ds
    rmap = lambda s: (blk0 + s, 0)
    past = cache_ckv.shape[2]
    return pl.pallas_call(
        functools.partial(_attn_sample_kernel, dec_seq=ds),
        grid=(dims.dec_batch,),
        in_specs=[pl.BlockSpec((ds, MLA_HEADS * KV_RANK), rmap), pl.BlockSpec((ds, MLA_HEADS * ROPE_DIM), rmap),
                  pl.BlockSpec((1, 1, past, KV_RANK), lambda s: (layer, s, 0, 0)),
                  pl.BlockSpec((1, 1, past, ROPE_DIM), lambda s: (layer, s, 0, 0)),
                  pl.BlockSpec((ds, KV_RANK), rmap), pl.BlockSpec((ds, ROPE_DIM), rmap),
                  pl.BlockSpec(wuv.shape, lambda s: (0, 0, 0)),
                  pl.BlockSpec(memory_space=pl.ANY)],
        out_specs=pl.BlockSpec((ds, MLA_HEADS * V_DIM), rmap),
        out_shape=jax.ShapeDtypeStruct(attn_prev.shape, attn_prev.dtype),
        input_output_aliases={7: 0},
        compiler_params=_cparams(("parallel",), 48 * MIB),
        name="attn_sample",
    )(qlat, qrope, cache_ckv, cache_krope, cb, krb, wuv, attn_prev)


def _outproj_kernel(x_ref, lru_ref, attn_ref, wo_ref, g_ref, b_ref, o_ref, *, dn_alpha):
    mix = _dot(lru_ref[...], wo_ref[:LRU_WIDTH, :]) + _dot(attn_ref[...], wo_ref[LRU_WIDTH:, :])
    o_ref[...] = _layer_norm(dn_alpha * x_ref[...] + mix, g_ref[...], b_ref[...])


def _outproj(x, lru_out, attn, wo, g, b, dims, dn_alpha):
    n = dims.n_tok
    row = lambda w: pl.BlockSpec((TM, w), lambda i: (i, 0))
    full = lambda a: pl.BlockSpec(a.shape, lambda i: (0,) * a.ndim)
    return pl.pallas_call(
        functools.partial(_outproj_kernel, dn_alpha=dn_alpha),
        grid=(n // TM,),
        in_specs=[row(D_MODEL), row(LRU_WIDTH), row(MLA_HEADS * V_DIM), full(wo), full(g), full(b)],
        out_specs=row(D_MODEL),
        out_shape=jax.ShapeDtypeStruct((n, D_MODEL), F32),
        compiler_params=_cparams(("parallel",), 40 * MIB),
        name="outproj",
    )(x, lru_out, attn, wo, g, b)


def _split_bf16(v):
    hi = v.astype(BF16)
    return hi, (v - hi.astype(F32)).astype(BF16)


def _route_kernel(x_ref, wr_ref, br_ref, tri_ref, low_ref, row_ref, col_ref, cnt_ref):
    x_hi, x_lo = _split_bf16(x_ref[...])
    w_hi, w_lo = _split_bf16(wr_ref[...])
    logits = _dot_nt(w_hi, x_hi) + (_dot_nt(w_hi, x_lo) + _dot_nt(w_lo, x_hi)) + br_ref[...]
    eidx = lax.broadcasted_iota(I32, logits.shape, 0)
    work = logits
    onehots, vals = [], []
    for _ in range(TOP_K):
        m = jnp.max(work, axis=0, keepdims=True)
        first = jnp.min(jnp.where(work == m, eidx, N_EXPERTS), axis=0, keepdims=True)
        oh = eidx == first
        onehots.append(oh)
        vals.append(m)
        work = jnp.where(oh, -jnp.inf, work)
    ex = [jnp.exp(v - vals[0]) for v in vals]
    denom = ex[0] + ex[1] + ex[2] + ex[3]
    sel = onehots[0] | onehots[1] | onehots[2] | onehots[3]
    self32 = sel.astype(F32)
    rank = _dot(self32.astype(BF16), tri_ref[...])
    cnt = jnp.sum(self32, axis=1, keepdims=True)
    padded = jnp.floor((cnt + (CH - 1)) * (1.0 / CH)) * CH
    cnt_ref[0] = jnp.broadcast_to(cnt, cnt_ref.shape[1:])
    lstart = _dot(low_ref[...], jnp.broadcast_to(padded, (N_EXPERTS, V7X_LANES)).astype(BF16))[:, 0:1]
    slotmat = lstart + rank
    rows = [jnp.sum(jnp.where(oh, slotmat, 0.0), axis=0, keepdims=True) for oh in onehots]
    rows += [e / denom for e in ex]
    info = jnp.concatenate(rows, axis=0)
    row_ref[0] = info
    pad = jnp.zeros((V7X_LANES - 2 * TOP_K, info.shape[1]), F32)
    col_ref[0] = jnp.concatenate([info, pad], axis=0).T


def _route(x1, wr_t, br, tri, low, dims):
    n_rt = dims.n_rt
    full = lambda a: pl.BlockSpec(a.shape, lambda i: (0,) * a.ndim)
    return pl.pallas_call(
        _route_kernel,
        grid=(n_rt,),
        in_specs=[pl.BlockSpec((RT, D_MODEL), lambda i: (i, 0)), full(wr_t), full(br), full(tri), full(low)],
        out_specs=[pl.BlockSpec((1, 2 * TOP_K, RT), lambda i: (i, 0, 0)),
                   pl.BlockSpec((1, RT, V7X_LANES), lambda i: (i, 0, 0)),
                   pl.BlockSpec((1, N_EXPERTS, V7X_LANES), lambda i: (i, 0, 0))],
        out_shape=[jax.ShapeDtypeStruct((n_rt, 2 * TOP_K, RT), F32),
                   jax.ShapeDtypeStruct((n_rt, RT, V7X_LANES), F32),
                   jax.ShapeDtypeStruct((n_rt, N_EXPERTS, V7X_LANES), F32)],
        compiler_params=_cparams(("parallel",), 32 * MIB),
        name="route",
    )(x1, wr_t, br, tri, low)


def _routing_tables(cnt, dims):
    cpb = EBLK // CH
    nch = (cnt.astype(I32) + (CH - 1)) // CH
    tot = jnp.sum(nch, axis=0)
    nblk = (tot + cpb - 1) // cpb
    blk_end = jnp.cumsum(nblk)
    base = (blk_end - nblk) * cpb
    gstart = base[None, :] + jnp.cumsum(nch, axis=0) - nch
    blk_e = jnp.minimum(jnp.searchsorted(blk_end, jnp.arange(dims.n_eblk, dtype=I32), side="right"),
                        N_EXPERTS - 1).astype(I32)
    return dict(nch=nch.reshape(-1), gstart=gstart.astype(I32).reshape(-1),
                tail_start=(base + tot).astype(I32), tail_n=(nblk * cpb - tot).astype(I32),
                blk_e=blk_e, nb_used=blk_end[-1:].astype(I32))


def _group_chunks(nch_ref, gst_ref, tile, visit):
    l0 = jnp.int32(0)
    for e in range(N_EXPERTS):
        n = nch_ref[tile * N_EXPERTS + e]
        g0 = gst_ref[tile * N_EXPERTS + e]

        def body(c, carry, l0=l0, g0=g0):
            visit(l0 + c, g0 + c)
            return carry

        lax.fori_loop(0, n, body, 0)
        l0 = l0 + n
    return l0


def _chunk(ref, c):
    if isinstance(c, int):
        return ref.at[pl.ds(c * CH, CH)]
    return ref.at[pl.ds(pl.multiple_of(c * CH, CH), CH)]


def _dispatch_kernel(nch_ref, gst_ref, tls_ref, tln_ref, x_ref, row_ref, xs_ref, loc_sc, zero_sc, sem):
    i = pl.program_id(0)
    slots = row_ref[0]
    srow = lax.broadcasted_iota(I32, (SLOTS, RT), 0).astype(F32)
    perm = ((srow == slots[0:1]) | (srow == slots[1:2]) | (srow == slots[2:3]) | (srow == slots[3:4]))
    loc_sc[...] = _dot(perm.astype(F32).astype(BF16), x_ref[...].astype(BF16))

    def copy(lc, gc):
        return pltpu.make_async_copy(_chunk(loc_sc, lc), _chunk(xs_ref, gc), sem)

    total = _group_chunks(nch_ref, gst_ref, i, lambda lc, gc: copy(lc, gc).start())

    def wait(c, carry):
        copy(0, 0).wait()
        return carry

    lax.fori_loop(0, total, wait, 0)

    @pl.when(i == pl.num_programs(0) - 1)
    def _():
        zero_sc[...] = jnp.zeros(zero_sc.shape, F32)
        n_tail = jnp.int32(0)
        for e in range(N_EXPERTS):
            n = tln_ref[e]
            g0 = tls_ref[e]

            def body(c, carry, g0=g0):
                pltpu.make_async_copy(zero_sc, _chunk(xs_ref, g0 + c), sem).start()
                return carry

            lax.fori_loop(0, n, body, 0)
            n_tail = n_tail + n

        def wait_tail(c, carry):
            pltpu.make_async_copy(zero_sc, _chunk(xs_ref, 0), sem).wait()
            return carry

        lax.fori_loop(0, n_tail, wait_tail, 0)


def _dispatch(x1, rowinfo, tabs, dims):
    rows = dims.n_eblk * EBLK
    return pl.pallas_call(
        _dispatch_kernel,
        grid_spec=pltpu.PrefetchScalarGridSpec(
            num_scalar_prefetch=4,
            grid=(dims.n_rt,),
            in_specs=[pl.BlockSpec((RT, D_MODEL), lambda i, *_: (i, 0)),
                      pl.BlockSpec((1, 2 * TOP_K, RT), lambda i, *_: (i, 0, 0))],
            out_specs=pl.BlockSpec(memory_space=pl.ANY),
            scratch_shapes=[pltpu.VMEM((SLOTS, D_MODEL), F32), pltpu.VMEM((CH, D_MODEL), F32),
                            pltpu.SemaphoreType.DMA],
        ),
        out_shape=jax.ShapeDtypeStruct((rows, D_MODEL), F32),
        compiler_params=_cparams(("arbitrary",), 40 * MIB),
        name="dispatch",
    )(tabs["nch"], tabs["gstart"], tabs["tail_start"], tabs["tail_n"], x1, rowinfo)


def _expert_kernel(be_ref, nb_ref, xs_ref, wgu_ref, bgu_ref, wd_ref, bd_ref, y_ref, wgu_sc, wd_sc):
    b = pl.program_id(0)

    @pl.when(b < nb_ref[0])
    def _():
        @pl.when((b == 0) | (be_ref[b] != be_ref[jnp.maximum(b - 1, 0)]))
        def _():
            wgu_sc[...] = wgu_ref[0, 0].astype(BF16)
            wd_sc[...] = wd_ref[0, 0].astype(BF16)

        h = _dot(xs_ref[...].astype(BF16), wgu_sc[...]) + bgu_ref[0, 0]
        hg = jnp.minimum(h[:, :D_FF], SWIGLU_LIMIT)
        hl = jnp.clip(h[:, D_FF:], -SWIGLU_LIMIT, SWIGLU_LIMIT)
        act = hg * jax.nn.sigmoid(SWIGLU_ALPHA * hg) * (hl + 1.0)
        y_ref[...] = _dot(act.astype(BF16), wd_sc[...]) + bd_ref[0, 0]


def _experts(xs, w_gu, b_gu, w_down, b_down, tabs, layer, dims):
    blk = lambda b, be, nb: jnp.minimum(b, nb[0] - 1)
    wmap = lambda b, be, nb: (layer, be[blk(b, be, nb)], 0, 0)
    return pl.pallas_call(
        _expert_kernel,
        grid_spec=pltpu.PrefetchScalarGridSpec(
            num_scalar_prefetch=2,
            grid=(dims.n_eblk,),
            in_specs=[pl.BlockSpec((EBLK, D_MODEL), lambda b, be, nb: (blk(b, be, nb), 0)),
                      pl.BlockSpec((1, 1, D_MODEL, 2 * D_FF), wmap), pl.BlockSpec((1, 1, 1, 2 * D_FF), wmap),
                      pl.BlockSpec((1, 1, D_FF, D_MODEL), wmap), pl.BlockSpec((1, 1, 1, D_MODEL), wmap)],
            out_specs=pl.BlockSpec((EBLK, D_MODEL), lambda b, be, nb: (blk(b, be, nb), 0)),
            scratch_shapes=[pltpu.VMEM((D_MODEL, 2 * D_FF), BF16), pltpu.VMEM((D_FF, D_MODEL), BF16)],
        ),
        out_shape=jax.ShapeDtypeStruct(xs.shape, F32),
        compiler_params=_cparams(("arbitrary",), 54 * MIB),
        name="experts",
    )(tabs["blk_e"], tabs["nb_used"], xs, w_gu, b_gu, w_down, b_down)


def _combine_kernel(nch_ref, gst_ref, x_ref, col_ref, yb_ref, g_ref, b_ref, o_ref, loc_sc, sem, *, dn_alpha):
    i = pl.program_id(0)

    def copy(lc, gc):
        return pltpu.make_async_copy(_chunk(yb_ref, gc), _chunk(loc_sc, lc), sem)

    total = _group_chunks(nch_ref, gst_ref, i, lambda lc, gc: copy(lc, gc).start())

    def zero(c, carry):
        _chunk(loc_sc, c)[...] = jnp.zeros((CH, D_MODEL), F32)
        return carry

    lax.fori_loop(total, SLOTS // CH, zero, 0)

    info = col_ref[0]
    scol = lax.broadcasted_iota(I32, (RT, SLOTS), 1).astype(F32)
    gmat = jnp.zeros((RT, SLOTS), F32)
    for k in range(TOP_K):
        gmat = gmat + jnp.where(scol == info[:, k:k + 1], info[:, TOP_K + k:TOP_K + k + 1], 0.0)

    def wait(c, carry):
        copy(0, 0).wait()
        return carry

    lax.fori_loop(0, total, wait, 0)
    y = _dot(gmat.astype(BF16), loc_sc[...].astype(BF16))
    o_ref[...] = _layer_norm(dn_alpha * x_ref[...] + y, g_ref[...], b_ref[...])


def _combine(x1, colinfo, yb, g, b, tabs, dims, dn_alpha):
    n = dims.n_tok
    return pl.pallas_call(
        functools.partial(_combine_kernel, dn_alpha=dn_alpha),
        grid_spec=pltpu.PrefetchScalarGridSpec(
            num_scalar_prefetch=2,
            grid=(dims.n_rt,),
            in_specs=[pl.BlockSpec((RT, D_MODEL), lambda i, *_: (i, 0)),
                      pl.BlockSpec((1, RT, V7X_LANES), lambda i, *_: (i, 0, 0)),
                      pl.BlockSpec(memory_space=pl.ANY),
                      pl.BlockSpec(g.shape, lambda i, *_: (0, 0)), pl.BlockSpec(b.shape, lambda i, *_: (0, 0))],
            out_specs=pl.BlockSpec((RT, D_MODEL), lambda i, *_: (i, 0)),
            scratch_shapes=[pltpu.VMEM((SLOTS, D_MODEL), F32), pltpu.SemaphoreType.DMA],
        ),
        out_shape=jax.ShapeDtypeStruct((n, D_MODEL), F32),
        compiler_params=_cparams(("arbitrary",), 40 * MIB),
        name="combine",
    )(tabs["nch"], tabs["gstart"], x1, colinfo, yb, g, b)


def _swap_halves(w, group):
    shp = w.shape
    w = w.reshape(shp[:-1] + (shp[-1] // group, 2, group // 2))
    return w[..., ::-1, :].reshape(shp)


def _block_diag(blocks, per):
    h, a, b = blocks.shape
    grouped = blocks.reshape(h // per, per, a, b)
    out = jnp.zeros((h // per, per * a, per * b), blocks.dtype)
    for p in range(per):
        out = out.at[:, p * a:(p + 1) * a, p * b:(p + 1) * b].set(grouped[:, p])
    return out


def _rope_tables(dims):
    half = ROPE_DIM // 2
    pos = jnp.concatenate([jnp.arange(dims.seq, dtype=I32),
                           jnp.tile(dims.past + jnp.arange(dims.dec_seq, dtype=I32), dims.dec_batch)]).astype(F32)
    freqs = ROPE_THETA ** (-2.0 * jnp.arange(half, dtype=F32) / ROPE_DIM)
    ang = pos[:, None] * freqs[None, :]
    cos, sin = jnp.cos(ang), jnp.sin(ang)
    cos2 = jnp.concatenate([cos, cos], axis=1)
    sin2 = jnp.concatenate([-sin, sin], axis=1)
    tabk = jnp.concatenate([cos2, sin2], axis=1)
    tabq = jnp.concatenate([jnp.tile(cos2, (1, MLA_HEADS)), jnp.tile(sin2, (1, MLA_HEADS))], axis=1)
    return tabq, tabk


def _layer_weights(p, l):
    o4 = 2 * LRU_WIDTH + Q_RANK + KV_RANK
    w_in = p["w_in"][l]
    win = jnp.concatenate([w_in, _swap_halves(w_in[:, o4:], ROPE_DIM)], axis=1).astype(BF16)
    w_uq = p["w_uq"][l].reshape(Q_RANK, MLA_HEADS, NOPE_DIM + ROPE_DIM)
    uq_nope = w_uq[:, :, :NOPE_DIM].reshape(Q_RANK, -1)
    uq_rope = w_uq[:, :, NOPE_DIM:].reshape(Q_RANK, -1)
    wuq = jnp.concatenate([uq_nope, uq_rope, _swap_halves(uq_rope, ROPE_DIM)], axis=1).astype(BF16)
    wuk = _block_diag(jnp.transpose(p["w_uk"][l], (1, 2, 0)), MLA_HEADS // 2)
    return dict(
        win=win, wuq=wuq, wuk=wuk.astype(BF16),
        wuv=jnp.transpose(p["w_uv"][l], (1, 0, 2)).astype(BF16),
        qg=p["q_norm_g"][l][None, :], kvg=p["kv_norm_g"][l][None, :],
        conv_w=p["conv_w"][l], conv_b=p["conv_b"][l][None, :],
        wrg=_block_diag(p["w_rg"][l], 4).astype(BF16), b_rg=p["b_rg"][l][None, :],
        wig=_block_diag(p["w_ig"][l], 4).astype(BF16), b_ig=p["b_ig"][l][None, :],
        lam=p["lru_lambda"][l][None, :],
        wo=p["w_o"][l].astype(BF16), ln1_g=p["ln1_g"][l][None, :], ln1_b=p["ln1_b"][l][None, :],
        wr_t=p["w_router"][l].T, br=p["b_router"][l][:, None],
        ln2_g=p["ln2_g"][l][None, :], ln2_b=p["ln2_b"][l][None, :],
    )


def _trunk(x_prompt, x_sample, cache_ckv, cache_krope, state_conv, state_lru, p):
    depth = p["w_in"].shape[0]
    dims = Dims(depth, x_prompt.shape[0], x_prompt.shape[1], x_sample.shape[0], x_sample.shape[1],
                cache_ckv.shape[2])
    assert dims.n_sample == TM and dims.seq % TM == 0 and dims.dec_seq == CHUNK and dims.n_tok % RT == 0
    dn_alpha = (2.0 * depth) ** 0.25
    x = jnp.concatenate([x_prompt.reshape(-1, D_MODEL), x_sample.reshape(-1, D_MODEL)], axis=0)
    tabq, tabk = _rope_tables(dims)
    tri = (jnp.arange(RT)[:, None] < jnp.arange(RT)[None, :]).astype(BF16)
    low = (jnp.arange(N_EXPERTS)[None, :] < jnp.arange(N_EXPERTS)[:, None]).astype(BF16)
    b_gu = p["b_gu"][:, :, None, :]
    b_down = p["b_down"][:, :, None, :]
    zero_state = jnp.zeros((dims.batch, V7X_SUBLANES, LRU_WIDTH), F32)
    tps = dims.seq // TM
    n_p = dims.n_prompt
    lo = V7X_SUBLANES - (CONV_W - 1)
    outs = {k: [] for k in ("ckv_p", "kr_p", "conv_p", "h_p", "ckv_s", "kr_s", "conv_s", "h_s")}
    for l in range(depth):
        lw = _layer_weights(p, l)
        ux, ug, c_new, cb, kr_new, krb, qlat, qrope = _inproj(
            x, lw["win"], tabq, tabk, lw["qg"], lw["kvg"], lw["wuq"], lw["wuk"], dims)
        lru_out, st_p = _lru(ux, ug, zero_state, lw, None,
                             seg=TM, n_seg=dims.batch * tps, tps=tps, row0=0)
        lru_out, st_s = _lru(ux, ug, _lru_state(state_conv[l], state_lru[l]), lw, lru_out,
                             seg=dims.dec_seq, n_seg=dims.dec_batch, tps=1, row0=n_p)
        attn = _attn_prompt(qlat, qrope, cb, krb, lw["wuv"], dims)
        attn = _attn_sample(qlat, qrope, cache_ckv, cache_krope, cb, krb, lw["wuv"], attn, l, dims)
        x1 = _outproj(x, lru_out, attn, lw["wo"], lw["ln1_g"], lw["ln1_b"], dims, dn_alpha)
        rowinfo, colinfo, cnt = _route(x1, lw["wr_t"], lw["br"], tri, low, dims)
        tabs = _routing_tables(cnt[:, :, 0], dims)
        xs = _dispatch(x1, rowinfo, tabs, dims)
        yb = _experts(xs, p["w_gu"], b_gu, p["w_down"], b_down, tabs, l, dims)
        x = _combine(x1, colinfo, yb, lw["ln2_g"], lw["ln2_b"], tabs, dims, dn_alpha)
        outs["ckv_p"].append(c_new[:n_p].reshape(dims.batch, dims.seq, KV_RANK))
        outs["kr_p"].append(kr_new[:n_p].reshape(dims.batch, dims.seq, ROPE_DIM))
        outs["conv_p"].append(st_p[:, lo:, :])
        outs["h_p"].append(st_p[:, 0, :])
        outs["ckv_s"].append(c_new[n_p:].reshape(dims.dec_batch, dims.dec_seq, KV_RANK))
        outs["kr_s"].append(kr_new[n_p:].reshape(dims.dec_batch, dims.dec_seq, ROPE_DIM))
        outs["conv_s"].append(st_s[:, lo:, :])
        outs["h_s"].append(st_s[:, 0, :])
    st = {k: jnp.stack(v) for k, v in outs.items()}
    y_prompt = x[:n_p].reshape(x_prompt.shape)
    y_sample = x[n_p:].reshape(x_sample.shape)
    return (y_prompt, y_sample, st["ckv_p"], st["kr_p"], st["conv_p"], st["h_p"],
            st["ckv_s"], st["kr_s"], st["conv_s"], st["h_s"])


def kernel(x_prompt, x_sample, cache_ckv, cache_krope, state_conv, state_lru, w_in, conv_w, conv_b, w_rg, b_rg,
           w_ig, b_ig, lru_lambda, q_norm_g, w_uq, kv_norm_g, w_uk, w_uv, w_o, ln1_g, ln1_b, w_router, b_router,
           w_gu, b_gu, w_down, b_down, ln2_g, ln2_b):
    p = dict(w_in=w_in, conv_w=conv_w, conv_b=conv_b, w_rg=w_rg, b_rg=b_rg, w_ig=w_ig, b_ig=b_ig,
             lru_lambda=lru_lambda, q_norm_g=q_norm_g, w_uq=w_uq, kv_norm_g=kv_norm_g, w_uk=w_uk, w_uv=w_uv,
             w_o=w_o, ln1_g=ln1_g, ln1_b=ln1_b, w_router=w_router, b_router=b_router, w_gu=w_gu, b_gu=b_gu,
             w_down=w_down, b_down=b_down, ln2_g=ln2_g, ln2_b=ln2_b)
    return _trunk(x_prompt, x_sample, cache_ckv, cache_krope, state_conv, state_lru, p)
```

```python
import functools
from typing import NamedTuple

import jax
import jax.numpy as jnp
from jax import lax
from jax.experimental import pallas as pl
from jax.experimental.pallas import tpu as pltpu

F32 = jnp.float32
BF16 = jnp.bfloat16
I32 = jnp.int32

D_MODEL = 1024
CHUNK = 64
LRU_WIDTH = 512
CONV_W = 4
LRU_C = 8.0
MLA_HEADS = 8
NOPE_DIM = 64
ROPE_DIM = 32
V_DIM = 64
Q_RANK = 384
KV_RANK = 256
ROPE_THETA = 10000.0
N_EXPERTS = 32
TOP_K = 4
D_FF = 1024
SWIGLU_LIMIT = 7.0
SWIGLU_ALPHA = 1.702
LN_EPS = 1e-5
RMS_EPS = 1e-6
LOG2_E = 1.4426950408889634

V7X_SUBLANES = 8
V7X_LANES = 128
V7X_VMEM_BYTES = 64 * 1024 * 1024
MIB = 1024 * 1024

TM = 512
TQ = 512
RT = 256
EBLK = 256
CH = V7X_SUBLANES
SLOTS = 1280
V_SLOT = 80
HEAD_SLOT = V7X_LANES

assert SLOTS >= RT * TOP_K + N_EXPERTS * (CH - 1) and SLOTS % CH == 0


class Dims(NamedTuple):
    depth: int
    batch: int
    seq: int
    dec_batch: int
    dec_seq: int
    past: int

    @property
    def n_prompt(self):
        return self.batch * self.seq

    @property
    def n_sample(self):
        return self.dec_batch * self.dec_seq

    @property
    def n_tok(self):
        return self.n_prompt + self.n_sample

    @property
    def n_rt(self):
        return self.n_tok // RT

    @property
    def n_eblk(self):
        chunks = self.n_tok * TOP_K // CH + self.n_rt * N_EXPERTS + N_EXPERTS * (EBLK // CH - 1)
        return -(-chunks // (EBLK // CH))


def _cparams(sem, nbytes):
    return pltpu.CompilerParams(dimension_semantics=sem,
                                vmem_limit_bytes=min(nbytes, V7X_VMEM_BYTES - 8 * MIB))


def _dot(a, b):
    return jnp.dot(a, b, preferred_element_type=F32)


def _dot_nt(a, b):
    return lax.dot_general(a, b, (((1,), (1,)), ((), ())), preferred_element_type=F32)


def _layer_norm(v, g, b):
    mu = jnp.mean(v, axis=-1, keepdims=True)
    vc = v - mu
    var = jnp.mean(vc * vc, axis=-1, keepdims=True)
    return vc * lax.rsqrt(var + LN_EPS) * g + b


def _rms_norm(v, g):
    return v * lax.rsqrt(jnp.mean(v * v, axis=-1, keepdims=True) + RMS_EPS) * g


def _inproj_kernel(x_ref, win_ref, tqa_ref, tqb_ref, tabk_ref, qg_ref, kvg_ref, wqa_ref, wqb_ref, wkx_ref,
                   ekr_ref, wvt_ref, vone_ref, ux_ref, ug_ref, c_ref, kr_ref, qx_ref, kx_ref, vt_ref):
    u = _dot(x_ref[...].astype(BF16), win_ref[...])
    o1, o2 = LRU_WIDTH, 2 * LRU_WIDTH
    o3 = o2 + Q_RANK
    o4 = o3 + KV_RANK
    ux_ref[...] = u[:, :o1]
    ug_ref[...] = u[:, o1:o2]
    c_new = _rms_norm(u[:, o3:o4], kvg_ref[...])
    c_ref[...] = c_new
    cb = c_new.astype(BF16)
    tk = tabk_ref[...]
    kr = (u[:, o4:o4 + ROPE_DIM] * tk[:, :ROPE_DIM]
          + u[:, o4 + ROPE_DIM:o4 + 2 * ROPE_DIM] * tk[:, ROPE_DIM:])
    kr_ref[...] = kr
    kn = _dot(cb, wkx_ref[...])
    kr_slot = _dot(kr.astype(BF16), ekr_ref[...])
    vt_ref[...] = (_dot_nt(wvt_ref[...], cb) + vone_ref[...]).astype(BF16)
    qn = _rms_norm(u[:, o2:o3], qg_ref[...]).astype(BF16)
    qa = _dot(qn, wqa_ref[...])
    qb = _dot(qn, wqb_ref[...])
    ta = tqa_ref[...]
    tb = tqb_ref[...]
    for h in range(MLA_HEADS):
        hs = slice(h * HEAD_SLOT, (h + 1) * HEAD_SLOT)
        kx_ref[:, hs] = (kn[:, hs] + kr_slot).astype(BF16)
        qx_ref[:, hs] = (qa[:, hs] * ta + qb[:, hs] * tb).astype(BF16)


def _inproj(x, lw, tabs, dims):
    n = dims.n_tok
    n_pt = dims.n_prompt // TM
    tps = dims.seq // TM
    tab_map = lambda i: (jnp.where(i < n_pt, i % tps, tps), 0)
    row = lambda w: pl.BlockSpec((TM, w), lambda i: (i, 0))
    full = lambda a: pl.BlockSpec(a.shape, lambda i: (0,) * a.ndim)
    tab = lambda a: pl.BlockSpec((TM, a.shape[1]), tab_map)
    weights = [lw["qg"], lw["kvg"], lw["wqa"], lw["wqb"], lw["wkx"], lw["ekr"], lw["wvt"], lw["vone"]]
    hw = MLA_HEADS * HEAD_SLOT
    return pl.pallas_call(
        _inproj_kernel,
        grid=(n // TM,),
        in_specs=[row(D_MODEL), full(lw["win"]), tab(tabs["tqa"]), tab(tabs["tqb"]), tab(tabs["tabk"])]
        + [full(w) for w in weights],
        out_specs=[row(LRU_WIDTH), row(LRU_WIDTH), row(KV_RANK), row(ROPE_DIM), row(hw), row(hw),
                   pl.BlockSpec((MLA_HEADS * V_SLOT, TM), lambda i: (0, i))],
        out_shape=[jax.ShapeDtypeStruct((n, LRU_WIDTH), F32), jax.ShapeDtypeStruct((n, LRU_WIDTH), F32),
                   jax.ShapeDtypeStruct((n, KV_RANK), F32), jax.ShapeDtypeStruct((n, ROPE_DIM), F32),
                   jax.ShapeDtypeStruct((n, hw), BF16), jax.ShapeDtypeStruct((n, hw), BF16),
                   jax.ShapeDtypeStruct((MLA_HEADS * V_SLOT, n), BF16)],
        compiler_params=_cparams(("parallel",), 48 * MIB),
        name="inproj",
    )(x, lw["win"], tabs["tqa"], tabs["tqb"], tabs["tabk"], *weights)


def _linear_scan(a, b, h0):
    n = a.shape[0]
    sub = lax.broadcasted_iota(I32, a.shape, 0) % V7X_SUBLANES
    d = 1
    while d < V7X_SUBLANES:
        keep = sub >= d
        a_sh = jnp.where(keep, pltpu.roll(a, d, 0), 1.0)
        b_sh = jnp.where(keep, pltpu.roll(b, d, 0), 0.0)
        b = b + a * b_sh
        a = a * a_sh
        d *= 2
    carry = h0
    groups = []
    for g in range(n // V7X_SUBLANES):
        rows = slice(g * V7X_SUBLANES, (g + 1) * V7X_SUBLANES)
        hg = b[rows] + a[rows] * carry
        groups.append(hg)
        carry = hg[V7X_SUBLANES - 1:V7X_SUBLANES]
    return jnp.concatenate(groups, axis=0)


def _gelu_tanh(v):
    return 0.5 * v * (1.0 + jnp.tanh(0.7978845608028654 * (v + 0.044715 * v * v * v)))


def _lru_kernel(ux_ref, ug_ref, st0_ref, cw_ref, cb_ref, wrg_ref, brg_ref, wig_ref, big_ref,
                lam_ref, out_ref, st_ref, xs_sc, carry_sc, *, seg, tps):
    t = pl.program_id(0) % tps
    lo = V7X_SUBLANES - (CONV_W - 1)

    @pl.when(t == 0)
    def _():
        carry_sc[...] = st0_ref[0]

    st_ref[0] = jnp.zeros(st_ref.shape[1:], F32)

    xs_sc[lo:V7X_SUBLANES, :] = carry_sc[lo:V7X_SUBLANES, :]
    xs_sc[V7X_SUBLANES:V7X_SUBLANES + seg, :] = ux_ref[...]
    cw = cw_ref[...]
    xc = cb_ref[...] + xs_sc[lo:lo + seg, :] * cw[0:1, :]
    for k in range(1, CONV_W):
        xc = xc + xs_sc[lo + k:lo + k + seg, :] * cw[k:k + 1, :]
    new_conv = xs_sc[seg + lo:seg + V7X_SUBLANES, :]
    carry_sc[lo:V7X_SUBLANES, :] = new_conv
    st_ref[0, lo:V7X_SUBLANES, :] = new_conv

    xcb = xc.astype(BF16)
    half = LRU_WIDTH // 2
    for g in range(2):
        cs = slice(g * half, (g + 1) * half)
        xg = xc[:, cs]
        r = jax.nn.sigmoid(_dot(xcb[:, cs], wrg_ref[g]) + brg_ref[:, cs])
        ig = jax.nn.sigmoid(_dot(xcb[:, cs], wig_ref[g]) + big_ref[:, cs])
        nl = -lam_ref[:, cs]
        softplus = jnp.maximum(nl, 0.0) + jnp.log1p(jnp.exp(-jnp.abs(nl)))
        log_a = -LRU_C * r * softplus
        a = jnp.exp(log_a)
        gap = 1.0 - a * a
        bt = jnp.where(gap > 0.0, gap * lax.rsqrt(gap), 0.0) * ig * xg
        h = _linear_scan(a, bt, carry_sc[0:1, cs])
        h_last = h[seg - 1:seg, :]
        carry_sc[0:1, cs] = h_last
        st_ref[0, 0:1, cs] = h_last
        out_ref[:, cs] = (_gelu_tanh(ug_ref[:, cs]) * h).astype(BF16)


def _lru_state(conv, h):
    lo = V7X_SUBLANES - (CONV_W - 1)
    return jnp.concatenate([h[:, None, :], jnp.zeros((h.shape[0], lo - 1, h.shape[1]), F32), conv], axis=1)


def _lru(ux, ug, st0, lw, prev_out, *, seg, n_seg, tps, row0):
    n = ux.shape[0]
    blk0 = row0 // seg
    row = pl.BlockSpec((seg, LRU_WIDTH), lambda i: (blk0 + i, 0))
    full = lambda a: pl.BlockSpec(a.shape, lambda i: (0,) * a.ndim)
    n_seq = n_seg // tps
    weights = [lw["conv_w"], lw["conv_b"], lw["wrg"], lw["b_rg"], lw["wig"], lw["b_ig"], lw["lam"]]
    in_specs = [row, row,
                pl.BlockSpec((1, V7X_SUBLANES, LRU_WIDTH), lambda i: (i // tps, 0, 0))] + [full(w) for w in weights]
    args = [ux, ug, st0] + weights
    aliases = {}
    if prev_out is not None:
        in_specs.append(pl.BlockSpec(memory_space=pl.ANY))
        args.append(prev_out)
        aliases = {len(args) - 1: 0}

    def body(*refs):
        refs = list(refs)
        if prev_out is not None:
            del refs[len(args) - 1]
        _lru_kernel(*refs, seg=seg, tps=tps)

    return pl.pallas_call(
        body,
        grid=(n_seg,),
        in_specs=in_specs,
        out_specs=[row, pl.BlockSpec((1, V7X_SUBLANES, LRU_WIDTH), lambda i: (i // tps, 0, 0))],
        out_shape=[jax.ShapeDtypeStruct((n, LRU_WIDTH), BF16),
                   jax.ShapeDtypeStruct((n_seq, V7X_SUBLANES, LRU_WIDTH), F32)],
        scratch_shapes=[pltpu.VMEM((seg + V7X_SUBLANES, LRU_WIDTH), F32),
                        pltpu.VMEM((V7X_SUBLANES, LRU_WIDTH), F32)],
        input_output_aliases=aliases,
        compiler_params=_cparams(("arbitrary",), 40 * MIB),
        name="lru_seg%d" % seg,
    )(*args)


def _attn_prompt_kernel(qi_ref, kj_ref, qx_ref, kx_ref, vt_ref, o_ref, m_sc, acc_sc):
    pair = pl.program_id(1)
    i = qi_ref[pair]
    j = kj_ref[pair]

    @pl.when(j == 0)
    def _():
        m_sc[...] = jnp.full(m_sc.shape, -jnp.inf, F32)
        acc_sc[...] = jnp.zeros(acc_sc.shape, F32)

    def step(masked):
        if masked:
            krow = lax.broadcasted_iota(I32, (TQ, TQ), 0)
            qcol = lax.broadcasted_iota(I32, (TQ, TQ), 1)
            visible = krow < (qcol // CHUNK + 1) * CHUNK

        def scores(h):
            hs = slice(h * HEAD_SLOT, (h + 1) * HEAD_SLOT)
            return _dot_nt(kx_ref[:, hs], qx_ref[:, hs])

        s_next = scores(0)
        for h in range(MLA_HEADS):
            s = s_next
            if h + 1 < MLA_HEADS:
                s_next = scores(h + 1)
            if masked:
                s = jnp.where(visible, s, -jnp.inf)
            m_prev = m_sc[h]
            m_new = jnp.maximum(m_prev, jnp.max(s, axis=0, keepdims=True))
            alpha = jnp.exp2(m_prev - m_new)
            p = jnp.exp2(s - m_new).astype(BF16)
            acc_sc[h] = alpha * acc_sc[h] + _dot(vt_ref[h * V_SLOT:(h + 1) * V_SLOT, :], p)
            m_sc[h] = m_new

    @pl.when(j < i)
    def _():
        step(False)

    @pl.when(j == i)
    def _():
        step(True)
        o_t = jnp.concatenate([acc_sc[h, :V_DIM, :] / acc_sc[h, V_DIM:V_DIM + 1, :] for h in range(MLA_HEADS)],
                              axis=0)
        o_ref[...] = o_t.T.astype(BF16)


def _attn_prompt(qx, kx, vt, dims):
    n = dims.n_tok
    tps = dims.seq // TQ
    pairs = [(i, j) for i in range(tps) for j in range(i + 1)]
    qi = jnp.asarray([p[0] for p in pairs], I32)
    kj = jnp.asarray([p[1] for p in pairs], I32)
    hw = MLA_HEADS * HEAD_SLOT
    return pl.pallas_call(
        _attn_prompt_kernel,
        grid_spec=pltpu.PrefetchScalarGridSpec(
            num_scalar_prefetch=2,
            grid=(dims.batch, len(pairs)),
            in_specs=[pl.BlockSpec((TQ, hw), lambda b, p, qi, kj: (b * tps + qi[p], 0)),
                      pl.BlockSpec((TQ, hw), lambda b, p, qi, kj: (b * tps + kj[p], 0)),
                      pl.BlockSpec((MLA_HEADS * V_SLOT, TQ), lambda b, p, qi, kj: (0, b * tps + kj[p]))],
            out_specs=pl.BlockSpec((TQ, MLA_HEADS * V_DIM), lambda b, p, qi, kj: (b * tps + qi[p], 0)),
            scratch_shapes=[pltpu.VMEM((MLA_HEADS, 1, TQ), F32), pltpu.VMEM((MLA_HEADS, V_SLOT, TQ), F32)],
        ),
        out_shape=jax.ShapeDtypeStruct((n, MLA_HEADS * V_DIM), BF16),
        compiler_params=_cparams(("parallel", "arbitrary"), 48 * MIB),
        name="attn_prompt",
    )(qi, kj, qx, kx, vt)


def _attn_sample_kernel(qx_ref, pk_ref, pkr_ref, c_ref, kr_ref, wabs_ref, ekr_ref, wuv_ref, prev_ref, o_ref,
                        *, dec_seq):
    del prev_ref
    slots = [qx_ref[:, h * HEAD_SLOT:(h + 1) * HEAD_SLOT] for h in range(MLA_HEADS)]
    q_slot = jnp.concatenate(slots, axis=0)
    q_lat = jnp.concatenate([_dot(slots[h], wabs_ref[h]) for h in range(MLA_HEADS)], axis=0).astype(BF16)
    ekr = ekr_ref[...]
    pk = pk_ref[0, 0].astype(BF16)
    pkr = _dot(pkr_ref[0, 0].astype(BF16), ekr).astype(BF16)
    k = c_ref[...].astype(BF16)
    kr = _dot(kr_ref[...].astype(BF16), ekr).astype(BF16)
    s_past = _dot_nt(q_lat, pk) + _dot_nt(q_slot, pkr)
    s_new = _dot_nt(q_lat, k) + _dot_nt(q_slot, kr)
    m = jnp.maximum(jnp.max(s_past, axis=1, keepdims=True), jnp.max(s_new, axis=1, keepdims=True))
    p_past = jnp.exp2(s_past - m)
    p_new = jnp.exp2(s_new - m)
    denom = jnp.sum(p_past, axis=1, keepdims=True) + jnp.sum(p_new, axis=1, keepdims=True)
    o_lat = (_dot(p_past.astype(BF16), pk) + _dot(p_new.astype(BF16), k)) / denom
    for h in range(MLA_HEADS):
        o_h = o_lat[h * dec_seq:(h + 1) * dec_seq, :].astype(BF16)
        o_ref[:, h * V_DIM:(h + 1) * V_DIM] = _dot(o_h, wuv_ref[h]).astype(BF16)


def _attn_sample(qx, cache_ckv, cache_krope, c_new, kr_new, lw, attn_prev, layer, dims):
    ds = dims.dec_seq
    blk0 = dims.n_prompt // ds
    rmap = lambda s: (blk0 + s, 0)
    past = cache_ckv.shape[2]
    full = lambda a: pl.BlockSpec(a.shape, lambda s: (0,) * a.ndim)
    return pl.pallas_call(
        functools.partial(_attn_sample_kernel, dec_seq=ds),
        grid=(dims.dec_batch,),
        in_specs=[pl.BlockSpec((ds, MLA_HEADS * HEAD_SLOT), rmap),
                  pl.BlockSpec((1, 1, past, KV_RANK), lambda s: (layer, s, 0, 0)),
                  pl.BlockSpec((1, 1, past, ROPE_DIM), lambda s: (layer, s, 0, 0)),
                  pl.BlockSpec((ds, KV_RANK), rmap), pl.BlockSpec((ds, ROPE_DIM), rmap),
                  full(lw["wabs"]), full(lw["ekr"]), full(lw["wuv"]),
                  pl.BlockSpec(memory_space=pl.ANY)],
        out_specs=pl.BlockSpec((ds, MLA_HEADS * V_DIM), rmap),
        out_shape=jax.ShapeDtypeStruct(attn_prev.shape, attn_prev.dtype),
        input_output_aliases={8: 0},
        compiler_params=_cparams(("parallel",), 48 * MIB),
        name="attn_sample",
    )(qx, cache_ckv, cache_krope, c_new, kr_new, lw["wabs"], lw["ekr"], lw["wuv"], attn_prev)


def _outproj_kernel(x_ref, lru_ref, attn_ref, wo_ref, g_ref, b_ref, o_ref, *, dn_alpha):
    mix = _dot(lru_ref[...], wo_ref[:LRU_WIDTH, :]) + _dot(attn_ref[...], wo_ref[LRU_WIDTH:, :])
    o_ref[...] = _layer_norm(dn_alpha * x_ref[...] + mix, g_ref[...], b_ref[...])


def _outproj(x, lru_out, attn, wo, g, b, dims, dn_alpha):
    n = dims.n_tok
    row = lambda w: pl.BlockSpec((TM, w), lambda i: (i, 0))
    full = lambda a: pl.BlockSpec(a.shape, lambda i: (0,) * a.ndim)
    return pl.pallas_call(
        functools.partial(_outproj_kernel, dn_alpha=dn_alpha),
        grid=(n // TM,),
        in_specs=[row(D_MODEL), row(LRU_WIDTH), row(MLA_HEADS * V_DIM), full(wo), full(g), full(b)],
        out_specs=row(D_MODEL),
        out_shape=jax.ShapeDtypeStruct((n, D_MODEL), F32),
        compiler_params=_cparams(("parallel",), 40 * MIB),
        name="outproj",
    )(x, lru_out, attn, wo, g, b)


def _split_bf16(v):
    hi = v.astype(BF16)
    return hi, (v - hi.astype(F32)).astype(BF16)


def _route_kernel(x_ref, wr_ref, br_ref, tri_ref, low_ref, row_ref, col_ref, cnt_ref):
    x_hi, x_lo = _split_bf16(x_ref[...])
    w_hi, w_lo = _split_bf16(wr_ref[...])
    logits = _dot_nt(w_hi, x_hi) + (_dot_nt(w_hi, x_lo) + _dot_nt(w_lo, x_hi)) + br_ref[...]
    eidx = lax.broadcasted_iota(I32, logits.shape, 0)
    work = logits
    onehots, vals = [], []
    for _ in range(TOP_K):
        m = jnp.max(work, axis=0, keepdims=True)
        first = jnp.min(jnp.where(work == m, eidx, N_EXPERTS), axis=0, keepdims=True)
        oh = eidx == first
        onehots.append(oh)
        vals.append(m)
        work = jnp.where(oh, -jnp.inf, work)
    ex = [jnp.exp(v - vals[0]) for v in vals]
    denom = ex[0] + ex[1] + ex[2] + ex[3]
    sel = onehots[0] | onehots[1] | onehots[2] | onehots[3]
    self32 = sel.astype(F32)
    rank = _dot(self32.astype(BF16), tri_ref[...])
    cnt = jnp.sum(self32, axis=1, keepdims=True)
    padded = jnp.floor((cnt + (CH - 1)) * (1.0 / CH)) * CH
    cnt_ref[0] = jnp.broadcast_to(cnt, cnt_ref.shape[1:])
    lstart = _dot(low_ref[...], jnp.broadcast_to(padded, (N_EXPERTS, V7X_LANES)).astype(BF16))[:, 0:1]
    slotmat = lstart + rank
    rows = [jnp.sum(jnp.where(oh, slotmat, 0.0), axis=0, keepdims=True) for oh in onehots]
    rows += [e / denom for e in ex]
    info = jnp.concatenate(rows, axis=0)
    row_ref[0] = info
    pad = jnp.zeros((V7X_LANES - 2 * TOP_K, info.shape[1]), F32)
    col_ref[0] = jnp.concatenate([info, pad], axis=0).T


def _route(x1, wr_t, br, tri, low, dims):
    n_rt = dims.n_rt
    full = lambda a: pl.BlockSpec(a.shape, lambda i: (0,) * a.ndim)
    return pl.pallas_call(
        _route_kernel,
        grid=(n_rt,),
        in_specs=[pl.BlockSpec((RT, D_MODEL), lambda i: (i, 0)), full(wr_t), full(br), full(tri), full(low)],
        out_specs=[pl.BlockSpec((1, 2 * TOP_K, RT), lambda i: (i, 0, 0)),
                   pl.BlockSpec((1, RT, V7X_LANES), lambda i: (i, 0, 0)),
                   pl.BlockSpec((1, N_EXPERTS, V7X_LANES), lambda i: (i, 0, 0))],
        out_shape=[jax.ShapeDtypeStruct((n_rt, 2 * TOP_K, RT), F32),
                   jax.ShapeDtypeStruct((n_rt, RT, V7X_LANES), F32),
                   jax.ShapeDtypeStruct((n_rt, N_EXPERTS, V7X_LANES), F32)],
        compiler_params=_cparams(("parallel",), 32 * MIB),
        name="route",
    )(x1, wr_t, br, tri, low)


def _routing_tables(cnt, dims):
    cpb = EBLK // CH
    nch = (cnt.astype(I32) + (CH - 1)) // CH
    tot = jnp.sum(nch, axis=0)
    nblk = (tot + cpb - 1) // cpb
    blk_end = jnp.cumsum(nblk)
    base = (blk_end - nblk) * cpb
    gstart = base[None, :] + jnp.cumsum(nch, axis=0) - nch
    blk_ids = jnp.arange(dims.n_eblk, dtype=I32)
    blk_e = jnp.minimum(jnp.sum((blk_end[None, :] <= blk_ids[:, None]).astype(I32), axis=1), N_EXPERTS - 1)
    return dict(nch=nch.reshape(-1), gstart=gstart.astype(I32).reshape(-1), ntot=jnp.sum(nch, axis=1).astype(I32),
                tail_start=(base + tot).astype(I32), tail_n=(nblk * cpb - tot).astype(I32),
                blk_e=blk_e, nb_used=blk_end[-1:].astype(I32))


def _group_chunks(nch_ref, gst_ref, tile, visit):
    l0 = jnp.int32(0)
    for e in range(N_EXPERTS):
        n = nch_ref[tile * N_EXPERTS + e]
        g0 = gst_ref[tile * N_EXPERTS + e]

        def body(c, carry, l0=l0, g0=g0):
            visit(l0 + c, g0 + c)
            return carry

        lax.fori_loop(0, n, body, 0)
        l0 = l0 + n
    return l0


def _chunk(ref, c):
    if isinstance(c, int):
        return ref.at[pl.ds(c * CH, CH)]
    return ref.at[pl.ds(pl.multiple_of(c * CH, CH), CH)]


def _wait_chunks(vmem_ref, hbm_ref, sem, count):
    @pl.when(count > 0)
    def _():
        rows = pl.ds(0, count * CH)
        pltpu.make_async_copy(vmem_ref.at[rows], hbm_ref.at[rows], sem).wait()


def _dispatch_kernel(nch_ref, gst_ref, ntot_ref, tls_ref, tln_ref, x_ref, row_ref, xs_ref, loc_sc, zero_sc, sems):
    i = pl.program_id(0)
    last = pl.num_programs(0) - 1
    slot = i % 2

    @pl.when(i >= 2)
    def _():
        _wait_chunks(loc_sc.at[slot], xs_ref, sems.at[slot], ntot_ref[jnp.maximum(i - 2, 0)])

    slots = row_ref[0]
    srow = lax.broadcasted_iota(I32, (SLOTS, RT), 0).astype(F32)
    perm = ((srow == slots[0:1]) | (srow == slots[1:2]) | (srow == slots[2:3]) | (srow == slots[3:4]))
    loc_sc[slot] = _dot(perm.astype(F32).astype(BF16), x_ref[...].astype(BF16))
    loc = loc_sc.at[slot]
    _group_chunks(nch_ref, gst_ref, i,
                  lambda lc, gc: pltpu.make_async_copy(_chunk(loc, lc), _chunk(xs_ref, gc), sems.at[slot]).start())

    @pl.when(i == last)
    def _():
        _wait_chunks(loc, xs_ref, sems.at[slot], ntot_ref[i])

        @pl.when(i >= 1)
        def _():
            _wait_chunks(loc_sc.at[1 - slot], xs_ref, sems.at[1 - slot], ntot_ref[jnp.maximum(i - 1, 0)])

        zero_sc[...] = jnp.zeros(zero_sc.shape, F32)
        n_tail = jnp.int32(0)
        for e in range(N_EXPERTS):
            n = tln_ref[e]
            g0 = tls_ref[e]

            def body(c, carry, g0=g0):
                pltpu.make_async_copy(zero_sc, _chunk(xs_ref, g0 + c), sems.at[0]).start()
                return carry

            lax.fori_loop(0, n, body, 0)
            n_tail = n_tail + n

        def wait_tail(c, carry):
            pltpu.make_async_copy(zero_sc, _chunk(xs_ref, 0), sems.at[0]).wait()
            return carry

        lax.fori_loop(0, n_tail, wait_tail, 0)


def _dispatch(x1, rowinfo, tabs, dims):
    rows = dims.n_eblk * EBLK
    return pl.pallas_call(
        _dispatch_kernel,
        grid_spec=pltpu.PrefetchScalarGridSpec(
            num_scalar_prefetch=5,
            grid=(dims.n_rt,),
            in_specs=[pl.BlockSpec((RT, D_MODEL), lambda i, *_: (i, 0)),
                      pl.BlockSpec((1, 2 * TOP_K, RT), lambda i, *_: (i, 0, 0))],
            out_specs=pl.BlockSpec(memory_space=pl.ANY),
            scratch_shapes=[pltpu.VMEM((2, SLOTS, D_MODEL), F32), pltpu.VMEM((CH, D_MODEL), F32),
                            pltpu.SemaphoreType.DMA((2,))],
        ),
        out_shape=jax.ShapeDtypeStruct((rows, D_MODEL), F32),
        compiler_params=_cparams(("arbitrary",), 40 * MIB),
        name="dispatch",
    )(tabs["nch"], tabs["gstart"], tabs["ntot"], tabs["tail_start"], tabs["tail_n"], x1, rowinfo)


def _expert_kernel(be_ref, nb_ref, xs_ref, wgu_ref, bgu_ref, wd_ref, bd_ref, y_ref, wgu_sc, wd_sc):
    b = pl.program_id(0)

    @pl.when(b < nb_ref[0])
    def _():
        @pl.when((b == 0) | (be_ref[b] != be_ref[jnp.maximum(b - 1, 0)]))
        def _():
            wgu_sc[...] = wgu_ref[0, 0].astype(BF16)
            wd_sc[...] = wd_ref[0, 0].astype(BF16)

        h = _dot(xs_ref[...].astype(BF16), wgu_sc[...]) + bgu_ref[0, 0]
        hg = jnp.minimum(h[:, :D_FF], SWIGLU_LIMIT)
        hl = jnp.clip(h[:, D_FF:], -SWIGLU_LIMIT, SWIGLU_LIMIT)
        act = hg * jax.nn.sigmoid(SWIGLU_ALPHA * hg) * (hl + 1.0)
        y_ref[...] = _dot(act.astype(BF16), wd_sc[...]) + bd_ref[0, 0]


def _experts(xs, w_gu, b_gu, w_down, b_down, tabs, layer, dims):
    blk = lambda b, be, nb: jnp.minimum(b, nb[0] - 1)
    wmap = lambda b, be, nb: (layer, be[blk(b, be, nb)], 0, 0)
    return pl.pallas_call(
        _expert_kernel,
        grid_spec=pltpu.PrefetchScalarGridSpec(
            num_scalar_prefetch=2,
            grid=(dims.n_eblk,),
            in_specs=[pl.BlockSpec((EBLK, D_MODEL), lambda b, be, nb: (blk(b, be, nb), 0)),
                      pl.BlockSpec((1, 1, D_MODEL, 2 * D_FF), wmap), pl.BlockSpec((1, 1, 1, 2 * D_FF), wmap),
                      pl.BlockSpec((1, 1, D_FF, D_MODEL), wmap), pl.BlockSpec((1, 1, 1, D_MODEL), wmap)],
            out_specs=pl.BlockSpec((EBLK, D_MODEL), lambda b, be, nb: (blk(b, be, nb), 0)),
            scratch_shapes=[pltpu.VMEM((D_MODEL, 2 * D_FF), BF16), pltpu.VMEM((D_FF, D_MODEL), BF16)],
        ),
        out_shape=jax.ShapeDtypeStruct(xs.shape, F32),
        compiler_params=_cparams(("arbitrary",), 54 * MIB),
        name="experts",
    )(tabs["blk_e"], tabs["nb_used"], xs, w_gu, b_gu, w_down, b_down)


def _combine_kernel(nch_ref, gst_ref, ntot_ref, x_ref, col_ref, yb_ref, g_ref, b_ref, o_ref, loc_sc, sems,
                    *, dn_alpha):
    i = pl.program_id(0)
    slot = i % 2

    def fetch(tile, sl):
        loc = loc_sc.at[sl]
        total = _group_chunks(
            nch_ref, gst_ref, tile,
            lambda lc, gc: pltpu.make_async_copy(_chunk(yb_ref, gc), _chunk(loc, lc), sems.at[sl]).start())

        def zero(c, carry):
            _chunk(loc, c)[...] = jnp.zeros((CH, D_MODEL), F32)
            return carry

        lax.fori_loop(total, SLOTS // CH, zero, 0)

    @pl.when(i == 0)
    def _():
        fetch(i, slot)

    @pl.when(i + 1 < pl.num_programs(0))
    def _():
        fetch(i + 1, 1 - slot)

    info = col_ref[0]
    scol = lax.broadcasted_iota(I32, (RT, SLOTS), 1).astype(F32)
    gmat = jnp.zeros((RT, SLOTS), F32)
    for k in range(TOP_K):
        gmat = gmat + jnp.where(scol == info[:, k:k + 1], info[:, TOP_K + k:TOP_K + k + 1], 0.0)

    _wait_chunks(loc_sc.at[slot], yb_ref, sems.at[slot], ntot_ref[i])
    y = _dot(gmat.astype(BF16), loc_sc[slot].astype(BF16))
    o_ref[...] = _layer_norm(dn_alpha * x_ref[...] + y, g_ref[...], b_ref[...])


def _combine(x1, colinfo, yb, g, b, tabs, dims, dn_alpha):
    n = dims.n_tok
    return pl.pallas_call(
        functools.partial(_combine_kernel, dn_alpha=dn_alpha),
        grid_spec=pltpu.PrefetchScalarGridSpec(
            num_scalar_prefetch=3,
            grid=(dims.n_rt,),
            in_specs=[pl.BlockSpec((RT, D_MODEL), lambda i, *_: (i, 0)),
                      pl.BlockSpec((1, RT, V7X_LANES), lambda i, *_: (i, 0, 0)),
                      pl.BlockSpec(memory_space=pl.ANY),
                      pl.BlockSpec(g.shape, lambda i, *_: (0, 0)), pl.BlockSpec(b.shape, lambda i, *_: (0, 0))],
            out_specs=pl.BlockSpec((RT, D_MODEL), lambda i, *_: (i, 0)),
            scratch_shapes=[pltpu.VMEM((2, SLOTS, D_MODEL), F32), pltpu.SemaphoreType.DMA((2,))],
        ),
        out_shape=jax.ShapeDtypeStruct((n, D_MODEL), F32),
        compiler_params=_cparams(("arbitrary",), 40 * MIB),
        name="combine",
    )(tabs["nch"], tabs["gstart"], tabs["ntot"], x1, colinfo, yb, g, b)


def _swap_halves(w, group):
    shp = w.shape
    w = w.reshape(shp[:-1] + (shp[-1] // group, 2, group // 2))
    return w[..., ::-1, :].reshape(shp)


def _block_diag(blocks, per):
    h, a, b = blocks.shape
    grouped = blocks.reshape(h // per, per, a, b)
    out = jnp.zeros((h // per, per * a, per * b), blocks.dtype)
    for p in range(per):
        out = out.at[:, p * a:(p + 1) * a, p * b:(p + 1) * b].set(grouped[:, p])
    return out


def _rope_tables(dims):
    half = ROPE_DIM // 2
    pos = jnp.concatenate([jnp.arange(dims.seq, dtype=I32),
                           jnp.tile(dims.past + jnp.arange(dims.dec_seq, dtype=I32), dims.dec_batch)]).astype(F32)
    freqs = ROPE_THETA ** (-2.0 * jnp.arange(half, dtype=F32) / ROPE_DIM)
    ang = pos[:, None] * freqs[None, :]
    cos, sin = jnp.cos(ang), jnp.sin(ang)
    cos2 = jnp.concatenate([cos, cos], axis=1)
    sin2 = jnp.concatenate([-sin, sin], axis=1)
    scale = (NOPE_DIM + ROPE_DIM) ** -0.5 * LOG2_E
    n = pos.shape[0]
    pad = jnp.zeros((n, HEAD_SLOT - NOPE_DIM - ROPE_DIM), F32)
    tqa = jnp.concatenate([jnp.full((n, NOPE_DIM), scale, F32), cos2 * scale, pad], axis=1)
    tqb = jnp.concatenate([jnp.zeros((n, NOPE_DIM), F32), sin2 * scale, pad], axis=1)
    return dict(tqa=tqa, tqb=tqb, tabk=jnp.concatenate([cos2, sin2], axis=1))


def _head_slots(nope, rope):
    r = nope.shape[0]
    pad = jnp.zeros((r, MLA_HEADS, HEAD_SLOT - NOPE_DIM - ROPE_DIM), nope.dtype)
    return jnp.concatenate([nope, rope, pad], axis=2).reshape(r, MLA_HEADS * HEAD_SLOT)


def _layer_weights(p, l):
    o4 = 2 * LRU_WIDTH + Q_RANK + KV_RANK
    w_in = p["w_in"][l]
    win = jnp.concatenate([w_in, _swap_halves(w_in[:, o4:], ROPE_DIM)], axis=1).astype(BF16)
    w_uq = p["w_uq"][l].reshape(Q_RANK, MLA_HEADS, NOPE_DIM + ROPE_DIM)
    uq_nope, uq_rope = w_uq[:, :, :NOPE_DIM], w_uq[:, :, NOPE_DIM:]
    w_uk = p["w_uk"][l]
    zero_rope = jnp.zeros((KV_RANK, MLA_HEADS, ROPE_DIM), F32)
    ekr = jnp.zeros((ROPE_DIM, HEAD_SLOT), F32).at[jnp.arange(ROPE_DIM), NOPE_DIM + jnp.arange(ROPE_DIM)].set(1.0)
    wabs = jnp.pad(jnp.transpose(w_uk, (1, 2, 0)), ((0, 0), (0, HEAD_SLOT - NOPE_DIM), (0, 0)))
    return dict(
        win=win,
        wqa=_head_slots(uq_nope, uq_rope).astype(BF16),
        wqb=_head_slots(jnp.zeros_like(uq_nope), _swap_halves(uq_rope, ROPE_DIM)).astype(BF16),
        wkx=_head_slots(w_uk, zero_rope).astype(BF16),
        ekr=ekr.astype(BF16),
        wvt=jnp.pad(jnp.transpose(p["w_uv"][l], (1, 2, 0)), ((0, 0), (0, V_SLOT - V_DIM), (0, 0))
                    ).reshape(MLA_HEADS * V_SLOT, KV_RANK).astype(BF16),
        vone=jnp.tile((jnp.arange(V_SLOT) == V_DIM).astype(F32), MLA_HEADS)[:, None],
        wabs=wabs.astype(BF16),
        wuv=jnp.transpose(p["w_uv"][l], (1, 0, 2)).astype(BF16),
        qg=p["q_norm_g"][l][None, :], kvg=p["kv_norm_g"][l][None, :],
        conv_w=p["conv_w"][l], conv_b=p["conv_b"][l][None, :],
        wrg=_block_diag(p["w_rg"][l], 4).astype(BF16), b_rg=p["b_rg"][l][None, :],
        wig=_block_diag(p["w_ig"][l], 4).astype(BF16), b_ig=p["b_ig"][l][None, :],
        lam=p["lru_lambda"][l][None, :],
        wo=p["w_o"][l].astype(BF16), ln1_g=p["ln1_g"][l][None, :], ln1_b=p["ln1_b"][l][None, :],
        wr_t=p["w_router"][l].T, br=p["b_router"][l][:, None],
        ln2_g=p["ln2_g"][l][None, :], ln2_b=p["ln2_b"][l][None, :],
    )


def _trunk(x_prompt, x_sample, cache_ckv, cache_krope, state_conv, state_lru, p):
    depth = p["w_in"].shape[0]
    dims = Dims(depth, x_prompt.shape[0], x_prompt.shape[1], x_sample.shape[0], x_sample.shape[1],
                cache_ckv.shape[2])
    assert dims.n_sample == TM and dims.seq % TM == 0 and dims.dec_seq == CHUNK and dims.n_tok % RT == 0
    dn_alpha = (2.0 * depth) ** 0.25
    x = jnp.concatenate([x_prompt.reshape(-1, D_MODEL), x_sample.reshape(-1, D_MODEL)], axis=0)
    rope_tabs = _rope_tables(dims)
    tri = (jnp.arange(RT)[:, None] < jnp.arange(RT)[None, :]).astype(BF16)
    low = (jnp.arange(N_EXPERTS)[None, :] < jnp.arange(N_EXPERTS)[:, None]).astype(BF16)
    b_gu = p["b_gu"][:, :, None, :]
    b_down = p["b_down"][:, :, None, :]
    zero_state = jnp.zeros((dims.batch, V7X_SUBLANES, LRU_WIDTH), F32)
    tps = dims.seq // TM
    n_p = dims.n_prompt
    lo = V7X_SUBLANES - (CONV_W - 1)
    outs = {k: [] for k in ("ckv_p", "kr_p", "conv_p", "h_p", "ckv_s", "kr_s", "conv_s", "h_s")}
    for l in range(depth):
        lw = _layer_weights(p, l)
        ux, ug, c_new, kr_new, qx, kx, vt = _inproj(x, lw, rope_tabs, dims)
        lru_out, st_p = _lru(ux, ug, zero_state, lw, None,
                             seg=TM, n_seg=dims.batch * tps, tps=tps, row0=0)
        lru_out, st_s = _lru(ux, ug, _lru_state(state_conv[l], state_lru[l]), lw, lru_out,
                             seg=dims.dec_seq, n_seg=dims.dec_batch, tps=1, row0=n_p)
        attn = _attn_prompt(qx, kx, vt, dims)
        attn = _attn_sample(qx, cache_ckv, cache_krope, c_new, kr_new, lw, attn, l, dims)
        x1 = _outproj(x, lru_out, attn, lw["wo"], lw["ln1_g"], lw["ln1_b"], dims, dn_alpha)
        rowinfo, colinfo, cnt = _route(x1, lw["wr_t"], lw["br"], tri, low, dims)
        tabs = _routing_tables(cnt[:, :, 0], dims)
        xs = _dispatch(x1, rowinfo, tabs, dims)
        yb = _experts(xs, p["w_gu"], b_gu, p["w_down"], b_down, tabs, l, dims)
        x = _combine(x1, colinfo, yb, lw["ln2_g"], lw["ln2_b"], tabs, dims, dn_alpha)
        outs["ckv_p"].append(c_new[:n_p].reshape(dims.batch, dims.seq, KV_RANK))
        outs["kr_p"].append(kr_new[:n_p].reshape(dims.batch, dims.seq, ROPE_DIM))
        outs["conv_p"].append(st_p[:, lo:, :])
        outs["h_p"].append(st_p[:, 0, :])
        outs["ckv_s"].append(c_new[n_p:].reshape(dims.dec_batch, dims.dec_seq, KV_RANK))
        outs["kr_s"].append(kr_new[n_p:].reshape(dims.dec_batch, dims.dec_seq, ROPE_DIM))
        outs["conv_s"].append(st_s[:, lo:, :])
        outs["h_s"].append(st_s[:, 0, :])
    st = {k: jnp.stack(v) for k, v in outs.items()}
    y_prompt = x[:n_p].reshape(x_prompt.shape)
    y_sample = x[n_p:].reshape(x_sample.shape)
    return (y_prompt, y_sample, st["ckv_p"], st["kr_p"], st["conv_p"], st["h_p"],
            st["ckv_s"], st["kr_s"], st["conv_s"], st["h_s"])


def kernel(x_prompt, x_sample, cache_ckv, cache_krope, state_conv, state_lru, w_in, conv_w, conv_b, w_rg, b_rg,
           w_ig, b_ig, lru_lambda, q_norm_g, w_uq, kv_norm_g, w_uk, w_uv, w_o, ln1_g, ln1_b, w_router, b_router,
           w_gu, b_gu, w_down, b_down, ln2_g, ln2_b):
    p = dict(w_in=w_in, conv_w=conv_w, conv_b=conv_b, w_rg=w_rg, b_rg=b_rg, w_ig=w_ig, b_ig=b_ig,
             lru_lambda=lru_lambda, q_norm_g=q_norm_g, w_uq=w_uq, kv_norm_g=kv_norm_g, w_uk=w_uk, w_uv=w_uv,
             w_o=w_o, ln1_g=ln1_g, ln1_b=ln1_b, w_router=w_router, b_router=b_router, w_gu=w_gu, b_gu=b_gu,
             w_down=w_down, b_down=b_down, ln2_g=ln2_g, ln2_b=ln2_b)
    return _trunk(x_prompt, x_sample, cache_ckv, cache_krope, state_conv, state_lru, p)
```

```python
import functools
from typing import NamedTuple

import jax
import jax.numpy as jnp
from jax import lax
from jax.experimental import pallas as pl
from jax.experimental.pallas import tpu as pltpu

F32 = jnp.float32
BF16 = jnp.bfloat16
I32 = jnp.int32

D_MODEL = 1024
CHUNK = 64
LRU_WIDTH = 512
CONV_W = 4
LRU_C = 8.0
MLA_HEADS = 8
NOPE_DIM = 64
ROPE_DIM = 32
V_DIM = 64
Q_RANK = 384
KV_RANK = 256
ROPE_THETA = 10000.0
N_EXPERTS = 32
TOP_K = 4
D_FF = 1024
SWIGLU_LIMIT = 7.0
SWIGLU_ALPHA = 1.702
LN_EPS = 1e-5
RMS_EPS = 1e-6
LOG2_E = 1.4426950408889634

V7X_SUBLANES = 8
V7X_LANES = 128
V7X_VMEM_BYTES = 64 * 1024 * 1024
MIB = 1024 * 1024

TM = 512
TQ = 512
TK = 1024
RT = 256
EBLK = 512
ESUB = 128
CH = V7X_SUBLANES
SLOTS = 1280
V_SLOT = 80
HEAD_SLOT = V7X_LANES

assert SLOTS >= RT * TOP_K + N_EXPERTS * (CH - 1) and SLOTS % CH == 0


class Dims(NamedTuple):
    depth: int
    batch: int
    seq: int
    dec_batch: int
    dec_seq: int
    past: int

    @property
    def n_prompt(self):
        return self.batch * self.seq

    @property
    def n_sample(self):
        return self.dec_batch * self.dec_seq

    @property
    def n_tok(self):
        return self.n_prompt + self.n_sample

    @property
    def n_rt(self):
        return self.n_tok // RT

    @property
    def n_eblk(self):
        chunks = self.n_tok * TOP_K // CH + self.n_rt * N_EXPERTS + N_EXPERTS * (EBLK // CH - 1)
        return -(-chunks // (EBLK // CH))


def _cparams(sem, nbytes):
    return pltpu.CompilerParams(dimension_semantics=sem,
                                vmem_limit_bytes=min(nbytes, V7X_VMEM_BYTES - 8 * MIB))


def _dot(a, b):
    return jnp.dot(a, b, preferred_element_type=F32)


def _dot_nt(a, b):
    return lax.dot_general(a, b, (((1,), (1,)), ((), ())), preferred_element_type=F32)


def _layer_norm(v, g, b):
    mu = jnp.mean(v, axis=-1, keepdims=True)
    vc = v - mu
    var = jnp.mean(vc * vc, axis=-1, keepdims=True)
    return vc * lax.rsqrt(var + LN_EPS) * g + b


def _rms_norm(v, g):
    return v * lax.rsqrt(jnp.mean(v * v, axis=-1, keepdims=True) + RMS_EPS) * g


def _inproj_kernel(xp_ref, xs_ref, win_ref, tqa_ref, tqb_ref, tabk_ref, qg_ref, kvg_ref, wqa_ref, wqb_ref, wkx_ref,
                   ekr_ref, wvt_ref, vone_ref, ux_ref, ug_ref, c_ref, kr_ref, qx_ref, kx_ref, vt_ref, *, n_pt):
    x = jnp.where(pl.program_id(0) >= n_pt, xs_ref[...], xp_ref[...])
    u = _dot(x.astype(BF16), win_ref[...])
    o1, o2 = LRU_WIDTH, 2 * LRU_WIDTH
    o3 = o2 + Q_RANK
    o4 = o3 + KV_RANK
    ux_ref[...] = u[:, :o1]
    ug_ref[...] = u[:, o1:o2]
    c_new = _rms_norm(u[:, o3:o4], kvg_ref[...])
    c_ref[...] = c_new
    cb = c_new.astype(BF16)
    tk = tabk_ref[...]
    kr = (u[:, o4:o4 + ROPE_DIM] * tk[:, :ROPE_DIM]
          + u[:, o4 + ROPE_DIM:o4 + 2 * ROPE_DIM] * tk[:, ROPE_DIM:])
    kr_ref[...] = kr
    kn = _dot(cb, wkx_ref[...])
    kr_slot = _dot(kr.astype(BF16), ekr_ref[...])
    vt_ref[...] = (_dot_nt(wvt_ref[...], cb) + vone_ref[...]).astype(BF16)
    qn = _rms_norm(u[:, o2:o3], qg_ref[...]).astype(BF16)
    qa = _dot(qn, wqa_ref[...])
    qb = _dot(qn, wqb_ref[...])
    ta = tqa_ref[...]
    tb = tqb_ref[...]
    for h in range(MLA_HEADS):
        hs = slice(h * HEAD_SLOT, (h + 1) * HEAD_SLOT)
        kx_ref[:, hs] = (kn[:, hs] + kr_slot).astype(BF16)
        qx_ref[:, hs] = (qa[:, hs] * ta + qb[:, hs] * tb).astype(BF16)


def _x_specs(x, n_pt):
    if isinstance(x, tuple):
        return list(x), [pl.BlockSpec((TM, D_MODEL), lambda i: (jnp.minimum(i, n_pt - 1), 0)),
                         pl.BlockSpec((TM, D_MODEL), lambda i: (0, 0))]
    return [x, x], [pl.BlockSpec((TM, D_MODEL), lambda i: (i, 0)), pl.BlockSpec((TM, D_MODEL), lambda i: (n_pt, 0))]


def _inproj(x, lw, tabs, dims):
    n = dims.n_tok
    n_pt = dims.n_prompt // TM
    x_args, x_specs = _x_specs(x, n_pt)
    tps = dims.seq // TM
    tab_map = lambda i: (jnp.where(i < n_pt, i % tps, tps), 0)
    row = lambda w: pl.BlockSpec((TM, w), lambda i: (i, 0))
    full = lambda a: pl.BlockSpec(a.shape, lambda i: (0,) * a.ndim)
    tab = lambda a: pl.BlockSpec((TM, a.shape[1]), tab_map)
    weights = [lw["qg"], lw["kvg"], lw["wqa"], lw["wqb"], lw["wkx"], lw["ekr"], lw["wvt"], lw["vone"]]
    hw = MLA_HEADS * HEAD_SLOT
    return pl.pallas_call(
        functools.partial(_inproj_kernel, n_pt=n_pt),
        grid=(n // TM,),
        in_specs=x_specs + [full(lw["win"]), tab(tabs["tqa"]), tab(tabs["tqb"]), tab(tabs["tabk"])]
        + [full(w) for w in weights],
        out_specs=[row(LRU_WIDTH), row(LRU_WIDTH), row(KV_RANK), row(ROPE_DIM), row(hw), row(hw),
                   pl.BlockSpec((MLA_HEADS * V_SLOT, TM), lambda i: (0, i))],
        out_shape=[jax.ShapeDtypeStruct((n, LRU_WIDTH), F32), jax.ShapeDtypeStruct((n, LRU_WIDTH), F32),
                   jax.ShapeDtypeStruct((n, KV_RANK), F32), jax.ShapeDtypeStruct((n, ROPE_DIM), F32),
                   jax.ShapeDtypeStruct((n, hw), BF16), jax.ShapeDtypeStruct((n, hw), BF16),
                   jax.ShapeDtypeStruct((MLA_HEADS * V_SLOT, n), BF16)],
        compiler_params=_cparams(("parallel",), 48 * MIB),
        name="inproj",
    )(*x_args, lw["win"], tabs["tqa"], tabs["tqb"], tabs["tabk"], *weights)


def _linear_scan(a, b, h0):
    n = a.shape[0]
    sub = lax.broadcasted_iota(I32, a.shape, 0) % V7X_SUBLANES
    d = 1
    while d < V7X_SUBLANES:
        keep = sub >= d
        a_sh = jnp.where(keep, pltpu.roll(a, d, 0), 1.0)
        b_sh = jnp.where(keep, pltpu.roll(b, d, 0), 0.0)
        b = b + a * b_sh
        a = a * a_sh
        d *= 2
    carry = h0
    groups = []
    for g in range(n // V7X_SUBLANES):
        rows = slice(g * V7X_SUBLANES, (g + 1) * V7X_SUBLANES)
        hg = b[rows] + a[rows] * carry
        groups.append(hg)
        carry = hg[V7X_SUBLANES - 1:V7X_SUBLANES]
    return jnp.concatenate(groups, axis=0)


def _gelu_tanh(v):
    return 0.5 * v * (1.0 + jnp.tanh(0.7978845608028654 * (v + 0.044715 * v * v * v)))


def _lru_kernel(ux_ref, ug_ref, st0_ref, cw_ref, cb_ref, wrg_ref, brg_ref, wig_ref, big_ref,
                lam_ref, out_ref, st_ref, xs_sc, carry_sc, *, seg, tps):
    t = pl.program_id(0) % tps
    lo = V7X_SUBLANES - (CONV_W - 1)

    @pl.when(t == 0)
    def _():
        carry_sc[...] = st0_ref[0]

    st_ref[0] = jnp.zeros(st_ref.shape[1:], F32)

    xs_sc[lo:V7X_SUBLANES, :] = carry_sc[lo:V7X_SUBLANES, :]
    xs_sc[V7X_SUBLANES:V7X_SUBLANES + seg, :] = ux_ref[...]
    cw = cw_ref[...]
    xc = cb_ref[...] + xs_sc[lo:lo + seg, :] * cw[0:1, :]
    for k in range(1, CONV_W):
        xc = xc + xs_sc[lo + k:lo + k + seg, :] * cw[k:k + 1, :]
    new_conv = xs_sc[seg + lo:seg + V7X_SUBLANES, :]
    carry_sc[lo:V7X_SUBLANES, :] = new_conv
    st_ref[0, lo:V7X_SUBLANES, :] = new_conv

    xcb = xc.astype(BF16)
    half = LRU_WIDTH // 2
    for g in range(2):
        cs = slice(g * half, (g + 1) * half)
        xg = xc[:, cs]
        r = jax.nn.sigmoid(_dot(xcb[:, cs], wrg_ref[g]) + brg_ref[:, cs])
        ig = jax.nn.sigmoid(_dot(xcb[:, cs], wig_ref[g]) + big_ref[:, cs])
        nl = -lam_ref[:, cs]
        softplus = jnp.maximum(nl, 0.0) + jnp.log1p(jnp.exp(-jnp.abs(nl)))
        log_a = -LRU_C * r * softplus
        a = jnp.exp(log_a)
        gap = 1.0 - a * a
        bt = jnp.where(gap > 0.0, gap * lax.rsqrt(gap), 0.0) * ig * xg
        h = _linear_scan(a, bt, carry_sc[0:1, cs])
        h_last = h[seg - 1:seg, :]
        carry_sc[0:1, cs] = h_last
        st_ref[0, 0:1, cs] = h_last
        out_ref[:, cs] = (_gelu_tanh(ug_ref[:, cs]) * h).astype(BF16)


def _lru_state(conv, h):
    lo = V7X_SUBLANES - (CONV_W - 1)
    return jnp.concatenate([h[:, None, :], jnp.zeros((h.shape[0], lo - 1, h.shape[1]), F32), conv], axis=1)


def _lru(ux, ug, st0, lw, prev_out, *, seg, n_seg, tps, row0):
    n = ux.shape[0]
    blk0 = row0 // seg
    row = pl.BlockSpec((seg, LRU_WIDTH), lambda i: (blk0 + i, 0))
    full = lambda a: pl.BlockSpec(a.shape, lambda i: (0,) * a.ndim)
    n_seq = n_seg // tps
    weights = [lw["conv_w"], lw["conv_b"], lw["wrg"], lw["b_rg"], lw["wig"], lw["b_ig"], lw["lam"]]
    in_specs = [row, row,
                pl.BlockSpec((1, V7X_SUBLANES, LRU_WIDTH), lambda i: (i // tps, 0, 0))] + [full(w) for w in weights]
    args = [ux, ug, st0] + weights
    aliases = {}
    if prev_out is not None:
        in_specs.append(pl.BlockSpec(memory_space=pl.ANY))
        args.append(prev_out)
        aliases = {len(args) - 1: 0}

    def body(*refs):
        refs = list(refs)
        if prev_out is not None:
            del refs[len(args) - 1]
        _lru_kernel(*refs, seg=seg, tps=tps)

    return pl.pallas_call(
        body,
        grid=(n_seg,),
        in_specs=in_specs,
        out_specs=[row, pl.BlockSpec((1, V7X_SUBLANES, LRU_WIDTH), lambda i: (i // tps, 0, 0))],
        out_shape=[jax.ShapeDtypeStruct((n, LRU_WIDTH), BF16),
                   jax.ShapeDtypeStruct((n_seq, V7X_SUBLANES, LRU_WIDTH), F32)],
        scratch_shapes=[pltpu.VMEM((seg + V7X_SUBLANES, LRU_WIDTH), F32),
                        pltpu.VMEM((V7X_SUBLANES, LRU_WIDTH), F32)],
        input_output_aliases=aliases,
        compiler_params=_cparams(("arbitrary",), 40 * MIB),
        name="lru_seg%d" % seg,
    )(*args)


def _attn_prompt_kernel(qi_ref, kj_ref, mk_ref, qx_ref, kx_ref, vt_ref, o_ref, m_sc, acc_sc):
    pair = pl.program_id(1)
    i = qi_ref[pair]
    j = kj_ref[pair]
    flag = mk_ref[pair]

    @pl.when(j == 0)
    def _():
        m_sc[...] = jnp.full(m_sc.shape, -jnp.inf, F32)
        acc_sc[...] = jnp.zeros(acc_sc.shape, F32)

    def step(masked):
        if masked:
            kpos = lax.broadcasted_iota(I32, (TK, TQ), 0) + j * TK
            qpos = lax.broadcasted_iota(I32, (TK, TQ), 1) + i * TQ
            visible = kpos < (qpos // CHUNK + 1) * CHUNK

        def scores(h):
            hs = slice(h * HEAD_SLOT, (h + 1) * HEAD_SLOT)
            return _dot_nt(kx_ref[:, hs], qx_ref[:, hs])

        pending = [scores(0), scores(1)]
        for h in range(MLA_HEADS):
            s = pending.pop(0)
            if h + 2 < MLA_HEADS:
                pending.append(scores(h + 2))
            if masked:
                s = jnp.where(visible, s, -jnp.inf)
            m_prev = m_sc[h]
            m_new = jnp.maximum(m_prev, jnp.max(s, axis=0, keepdims=True))
            alpha = jnp.exp2(m_prev - m_new)
            p = jnp.exp2(s - m_new).astype(BF16)
            acc_sc[h] = alpha * acc_sc[h] + _dot(vt_ref[h * V_SLOT:(h + 1) * V_SLOT, :], p)
            m_sc[h] = m_new

    @pl.when((flag & 1) == 0)
    def _():
        step(False)

    @pl.when((flag & 1) == 1)
    def _():
        step(True)

    @pl.when((flag & 2) == 2)
    def _():
        o_t = jnp.concatenate([acc_sc[h, :V_DIM, :] / acc_sc[h, V_DIM:V_DIM + 1, :] for h in range(MLA_HEADS)],
                              axis=0)
        o_ref[...] = o_t.T.astype(BF16)


def _attn_prompt(qx, kx, vt, dims):
    n = dims.n_tok
    nq = dims.seq // TQ
    nk = dims.seq // TK
    pairs = []
    for i in range(nq):
        n_kv = -(-(i + 1) * TQ // TK)
        for j in range(n_kv):
            pairs.append((i, j, int((j + 1) * TK > i * TQ) | (2 if j == n_kv - 1 else 0)))
    qi = jnp.asarray([p[0] for p in pairs], I32)
    kj = jnp.asarray([p[1] for p in pairs], I32)
    mk = jnp.asarray([p[2] for p in pairs], I32)
    hw = MLA_HEADS * HEAD_SLOT
    return pl.pallas_call(
        _attn_prompt_kernel,
        grid_spec=pltpu.PrefetchScalarGridSpec(
            num_scalar_prefetch=3,
            grid=(dims.batch, len(pairs)),
            in_specs=[pl.BlockSpec((TQ, hw), lambda b, p, qi, kj, mk: (b * nq + qi[p], 0)),
                      pl.BlockSpec((TK, hw), lambda b, p, qi, kj, mk: (b * nk + kj[p], 0)),
                      pl.BlockSpec((MLA_HEADS * V_SLOT, TK), lambda b, p, qi, kj, mk: (0, b * nk + kj[p]))],
            out_specs=pl.BlockSpec((TQ, MLA_HEADS * V_DIM), lambda b, p, qi, kj, mk: (b * nq + qi[p], 0)),
            scratch_shapes=[pltpu.VMEM((MLA_HEADS, 1, TQ), F32), pltpu.VMEM((MLA_HEADS, V_SLOT, TQ), F32)],
        ),
        out_shape=jax.ShapeDtypeStruct((n, MLA_HEADS * V_DIM), BF16),
        compiler_params=_cparams(("parallel", "arbitrary"), 48 * MIB),
        name="attn_prompt",
    )(qi, kj, mk, qx, kx, vt)


def _attn_sample_kernel(qx_ref, pk_ref, pkr_ref, c_ref, kr_ref, wabs_ref, ekr_ref, wuv_ref, prev_ref, o_ref,
                        *, dec_seq):
    del prev_ref
    slots = [qx_ref[:, h * HEAD_SLOT:(h + 1) * HEAD_SLOT] for h in range(MLA_HEADS)]
    q_slot = jnp.concatenate(slots, axis=0)
    q_lat = jnp.concatenate([_dot(slots[h], wabs_ref[h]) for h in range(MLA_HEADS)], axis=0).astype(BF16)
    ekr = ekr_ref[...]
    pk = pk_ref[0, 0].astype(BF16)
    pkr = _dot(pkr_ref[0, 0].astype(BF16), ekr).astype(BF16)
    k = c_ref[...].astype(BF16)
    kr = _dot(kr_ref[...].astype(BF16), ekr).astype(BF16)
    s_past = _dot_nt(q_lat, pk) + _dot_nt(q_slot, pkr)
    s_new = _dot_nt(q_lat, k) + _dot_nt(q_slot, kr)
    m = jnp.maximum(jnp.max(s_past, axis=1, keepdims=True), jnp.max(s_new, axis=1, keepdims=True))
    p_past = jnp.exp2(s_past - m)
    p_new = jnp.exp2(s_new - m)
    denom = jnp.sum(p_past, axis=1, keepdims=True) + jnp.sum(p_new, axis=1, keepdims=True)
    o_lat = (_dot(p_past.astype(BF16), pk) + _dot(p_new.astype(BF16), k)) / denom
    for h in range(MLA_HEADS):
        o_h = o_lat[h * dec_seq:(h + 1) * dec_seq, :].astype(BF16)
        o_ref[:, h * V_DIM:(h + 1) * V_DIM] = _dot(o_h, wuv_ref[h]).astype(BF16)


def _attn_sample(qx, cache_ckv, cache_krope, c_new, kr_new, lw, attn_prev, layer, dims):
    ds = dims.dec_seq
    blk0 = dims.n_prompt // ds
    rmap = lambda s: (blk0 + s, 0)
    past = cache_ckv.shape[2]
    full = lambda a: pl.BlockSpec(a.shape, lambda s: (0,) * a.ndim)
    return pl.pallas_call(
        functools.partial(_attn_sample_kernel, dec_seq=ds),
        grid=(dims.dec_batch,),
        in_specs=[pl.BlockSpec((ds, MLA_HEADS * HEAD_SLOT), rmap),
                  pl.BlockSpec((1, 1, past, KV_RANK), lambda s: (layer, s, 0, 0)),
                  pl.BlockSpec((1, 1, past, ROPE_DIM), lambda s: (layer, s, 0, 0)),
                  pl.BlockSpec((ds, KV_RANK), rmap), pl.BlockSpec((ds, ROPE_DIM), rmap),
                  full(lw["wabs"]), full(lw["ekr"]), full(lw["wuv"]),
                  pl.BlockSpec(memory_space=pl.ANY)],
        out_specs=pl.BlockSpec((ds, MLA_HEADS * V_DIM), rmap),
        out_shape=jax.ShapeDtypeStruct(attn_prev.shape, attn_prev.dtype),
        input_output_aliases={8: 0},
        compiler_params=_cparams(("parallel",), 48 * MIB),
        name="attn_sample",
    )(qx, cache_ckv, cache_krope, c_new, kr_new, lw["wabs"], lw["ekr"], lw["wuv"], attn_prev)


def _outproj_kernel(xp_ref, xs_ref, lru_ref, attn_ref, wo_ref, g_ref, b_ref, wr_ref, br_ref, tri_ref, low_ref,
                    o_ref, row_ref, col_ref, cnt_ref, *, dn_alpha, n_pt):
    x = jnp.where(pl.program_id(0) >= n_pt, xs_ref[...], xp_ref[...])
    mix = _dot(lru_ref[...], wo_ref[:LRU_WIDTH, :]) + _dot(attn_ref[...], wo_ref[LRU_WIDTH:, :])
    x1 = _layer_norm(dn_alpha * x + mix, g_ref[...], b_ref[...])
    o_ref[...] = x1
    for r in range(TM // RT):
        info, cnt = _route_tile(x1[r * RT:(r + 1) * RT, :], wr_ref[...], br_ref[...], tri_ref[...], low_ref[...])
        row_ref[r] = info
        pad = jnp.zeros((V7X_LANES - 2 * TOP_K, RT), F32)
        col_ref[r] = jnp.concatenate([info, pad], axis=0).T
        cnt_ref[r] = jnp.broadcast_to(cnt, cnt_ref.shape[1:])


def _outproj(x, lru_out, attn, lw, tri, low, dims, dn_alpha):
    n = dims.n_tok
    n_pt = dims.n_prompt // TM
    per = TM // RT
    x_args, x_specs = _x_specs(x, n_pt)
    row = lambda w: pl.BlockSpec((TM, w), lambda i: (i, 0))
    full = lambda a: pl.BlockSpec(a.shape, lambda i: (0,) * a.ndim)
    consts = [lw["wo"], lw["ln1_g"], lw["ln1_b"], lw["wr_t"], lw["br"], tri, low]
    return pl.pallas_call(
        functools.partial(_outproj_kernel, dn_alpha=dn_alpha, n_pt=n_pt),
        grid=(n // TM,),
        in_specs=x_specs + [row(LRU_WIDTH), row(MLA_HEADS * V_DIM)] + [full(c) for c in consts],
        out_specs=[row(D_MODEL),
                   pl.BlockSpec((per, 2 * TOP_K, RT), lambda i: (i, 0, 0)),
                   pl.BlockSpec((per, RT, V7X_LANES), lambda i: (i, 0, 0)),
                   pl.BlockSpec((per, N_EXPERTS, V7X_LANES), lambda i: (i, 0, 0))],
        out_shape=[jax.ShapeDtypeStruct((n, D_MODEL), F32),
                   jax.ShapeDtypeStruct((dims.n_rt, 2 * TOP_K, RT), F32),
                   jax.ShapeDtypeStruct((dims.n_rt, RT, V7X_LANES), F32),
                   jax.ShapeDtypeStruct((dims.n_rt, N_EXPERTS, V7X_LANES), F32)],
        compiler_params=_cparams(("parallel",), 40 * MIB),
        name="outproj",
    )(*x_args, lru_out, attn, *consts)


def _split_bf16(v):
    hi = v.astype(BF16)
    return hi, (v - hi.astype(F32)).astype(BF16)


def _route_tile(x, wr, br, tri, low):
    x_hi, x_lo = _split_bf16(x)
    w_hi, w_lo = _split_bf16(wr)
    logits = _dot_nt(w_hi, x_hi) + (_dot_nt(w_hi, x_lo) + _dot_nt(w_lo, x_hi)) + br
    eidx = lax.broadcasted_iota(I32, logits.shape, 0)
    work = logits
    onehots, vals = [], []
    for _ in range(TOP_K):
        m = jnp.max(work, axis=0, keepdims=True)
        first = jnp.min(jnp.where(work == m, eidx, N_EXPERTS), axis=0, keepdims=True)
        oh = eidx == first
        onehots.append(oh)
        vals.append(m)
        work = jnp.where(oh, -jnp.inf, work)
    ex = [jnp.exp(v - vals[0]) for v in vals]
    denom = ex[0] + ex[1] + ex[2] + ex[3]
    sel = onehots[0] | onehots[1] | onehots[2] | onehots[3]
    self32 = sel.astype(F32)
    rank = _dot(self32.astype(BF16), tri)
    cnt = jnp.sum(self32, axis=1, keepdims=True)
    padded = jnp.floor((cnt + (CH - 1)) * (1.0 / CH)) * CH
    lstart = _dot(low, jnp.broadcast_to(padded, (N_EXPERTS, V7X_LANES)).astype(BF16))[:, 0:1]
    slotmat = lstart + rank
    rows = [jnp.sum(jnp.where(oh, slotmat, 0.0), axis=0, keepdims=True) for oh in onehots]
    rows += [e / denom for e in ex]
    return jnp.concatenate(rows, axis=0), cnt


def _routing_tables(cnt, dims):
    cpb = EBLK // CH
    nch = (cnt.astype(I32) + (CH - 1)) // CH
    tot = jnp.sum(nch, axis=0)
    nblk = (tot + cpb - 1) // cpb
    blk_end = jnp.cumsum(nblk)
    base = (blk_end - nblk) * cpb
    gstart = base[None, :] + jnp.cumsum(nch, axis=0) - nch
    blk_ids = jnp.arange(dims.n_eblk, dtype=I32)
    blk_e = jnp.minimum(jnp.sum((blk_end[None, :] <= blk_ids[:, None]).astype(I32), axis=1), N_EXPERTS - 1)
    first_blk = (blk_end - nblk)[blk_e]
    blk_valid = jnp.clip(tot[blk_e] * CH - (blk_ids - first_blk) * EBLK, 0, EBLK).astype(I32)
    e_ids = jnp.arange(N_EXPERTS, dtype=I32)
    later_used = (e_ids[None, :] > e_ids[:, None]) & (nblk[None, :] > 0)
    next_e = jnp.min(jnp.where(later_used, e_ids[None, :], N_EXPERTS), axis=1)
    next_e = jnp.where(next_e < N_EXPERTS, next_e, -1).astype(I32)
    return dict(nch=nch.reshape(-1), gstart=gstart.astype(I32).reshape(-1), ntot=jnp.sum(nch, axis=1).astype(I32),
                next_e=next_e, blk_valid=blk_valid,
                tail_start=(base + tot).astype(I32), tail_n=(nblk * cpb - tot).astype(I32),
                blk_e=blk_e, nb_used=blk_end[-1:].astype(I32))


def _group_chunks(nch_ref, gst_ref, tile, visit):
    l0 = jnp.int32(0)
    for e in range(N_EXPERTS):
        n = nch_ref[tile * N_EXPERTS + e]
        g0 = gst_ref[tile * N_EXPERTS + e]

        def body(c, carry, l0=l0, g0=g0):
            visit(l0 + c, g0 + c)
            return carry

        lax.fori_loop(0, n, body, 0)
        l0 = l0 + n
    return l0


def _chunk(ref, c):
    if isinstance(c, int):
        return ref.at[pl.ds(c * CH, CH)]
    return ref.at[pl.ds(pl.multiple_of(c * CH, CH), CH)]


def _wait_chunks(vmem_ref, hbm_ref, sem, count):
    @pl.when(count > 0)
    def _():
        rows = pl.ds(0, count * CH)
        pltpu.make_async_copy(vmem_ref.at[rows], hbm_ref.at[rows], sem).wait()


def _dispatch_kernel(nch_ref, gst_ref, ntot_ref, tls_ref, tln_ref, x_ref, row_ref, xs_ref, loc_sc, zero_sc, sems):
    i = pl.program_id(0)
    last = pl.num_programs(0) - 1
    slot = i % 2

    @pl.when(i >= 2)
    def _():
        _wait_chunks(loc_sc.at[slot], xs_ref, sems.at[slot], ntot_ref[jnp.maximum(i - 2, 0)])

    slots = row_ref[0]
    srow = lax.broadcasted_iota(I32, (SLOTS, RT), 0).astype(F32)
    perm = ((srow == slots[0:1]) | (srow == slots[1:2]) | (srow == slots[2:3]) | (srow == slots[3:4]))
    loc_sc[slot] = _dot(perm.astype(F32).astype(BF16), x_ref[...].astype(BF16))
    loc = loc_sc.at[slot]
    _group_chunks(nch_ref, gst_ref, i,
                  lambda lc, gc: pltpu.make_async_copy(_chunk(loc, lc), _chunk(xs_ref, gc), sems.at[slot]).start())

    @pl.when(i == last)
    def _():
        _wait_chunks(loc, xs_ref, sems.at[slot], ntot_ref[i])

        @pl.when(i >= 1)
        def _():
            _wait_chunks(loc_sc.at[1 - slot], xs_ref, sems.at[1 - slot], ntot_ref[jnp.maximum(i - 1, 0)])

        zero_sc[...] = jnp.zeros(zero_sc.shape, F32)
        n_tail = jnp.int32(0)
        for e in range(N_EXPERTS):
            n = tln_ref[e]
            g0 = tls_ref[e]

            def body(c, carry, g0=g0):
                pltpu.make_async_copy(zero_sc, _chunk(xs_ref, g0 + c), sems.at[0]).start()
                return carry

            lax.fori_loop(0, n, body, 0)
            n_tail = n_tail + n

        def wait_tail(c, carry):
            pltpu.make_async_copy(zero_sc, _chunk(xs_ref, 0), sems.at[0]).wait()
            return carry

        lax.fori_loop(0, n_tail, wait_tail, 0)


def _dispatch(x1, rowinfo, tabs, dims):
    rows = dims.n_eblk * EBLK
    return pl.pallas_call(
        _dispatch_kernel,
        grid_spec=pltpu.PrefetchScalarGridSpec(
            num_scalar_prefetch=5,
            grid=(dims.n_rt,),
            in_specs=[pl.BlockSpec((RT, D_MODEL), lambda i, *_: (i, 0)),
                      pl.BlockSpec((1, 2 * TOP_K, RT), lambda i, *_: (i, 0, 0))],
            out_specs=pl.BlockSpec(memory_space=pl.ANY),
            scratch_shapes=[pltpu.VMEM((2, SLOTS, D_MODEL), F32), pltpu.VMEM((CH, D_MODEL), F32),
                            pltpu.SemaphoreType.DMA((2,))],
        ),
        out_shape=jax.ShapeDtypeStruct((rows, D_MODEL), F32),
        compiler_params=_cparams(("arbitrary",), 40 * MIB),
        name="dispatch",
    )(tabs["nch"], tabs["gstart"], tabs["ntot"], tabs["tail_start"], tabs["tail_n"], x1, rowinfo)


def _expert_kernel(be_ref, nb_ref, nxt_ref, bv_ref, xs_ref, wgu_hbm, bgu_ref, wd_hbm, bd_ref, y_ref,
                   wgu_st, wd_st, wgu_sc, wd_sc, sems, *, layer):
    b = pl.program_id(0)

    def copies(e):
        return (pltpu.make_async_copy(wgu_hbm.at[layer, e], wgu_st, sems.at[0]),
                pltpu.make_async_copy(wd_hbm.at[layer, e], wd_st, sems.at[1]))

    def mlp(n_rows):
        rows = slice(0, n_rows)
        h = _dot(xs_ref[rows, :].astype(BF16), wgu_sc[...]) + bgu_ref[0, 0]
        hg = jnp.minimum(h[:, :D_FF], SWIGLU_LIMIT)
        hl = jnp.clip(h[:, D_FF:], -SWIGLU_LIMIT, SWIGLU_LIMIT)
        act = hg * jax.nn.sigmoid(SWIGLU_ALPHA * hg) * (hl + 1.0)
        y_ref[rows, :] = _dot(act.astype(BF16), wd_sc[...]) + bd_ref[0, 0]

    @pl.when(b < nb_ref[0])
    def _():
        e = be_ref[b]

        @pl.when(b == 0)
        def _():
            for c in copies(e):
                c.start()

        @pl.when((b == 0) | (e != be_ref[jnp.maximum(b - 1, 0)]))
        def _():
            for c in copies(e):
                c.wait()
            wgu_sc[...] = wgu_st[...].astype(BF16)
            wd_sc[...] = wd_st[...].astype(BF16)
            nxt = nxt_ref[e]

            @pl.when(nxt >= 0)
            def _():
                for c in copies(nxt):
                    c.start()

        valid = bv_ref[b]
        for n_rows in range(ESUB, EBLK + 1, ESUB):
            @pl.when((valid > n_rows - ESUB) & (valid <= n_rows))
            def _(n_rows=n_rows):
                mlp(n_rows)


def _experts(xs, w_gu, b_gu, w_down, b_down, tabs, layer, dims):
    blk = lambda b, be, nb, *_: jnp.minimum(b, nb[0] - 1)
    wmap = lambda b, be, nb, *_: (layer, be[blk(b, be, nb)], 0, 0)
    row = pl.BlockSpec((EBLK, D_MODEL), lambda b, be, nb, *_: (blk(b, be, nb), 0))
    return pl.pallas_call(
        functools.partial(_expert_kernel, layer=layer),
        grid_spec=pltpu.PrefetchScalarGridSpec(
            num_scalar_prefetch=4,
            grid=(dims.n_eblk,),
            in_specs=[row, pl.BlockSpec(memory_space=pl.ANY), pl.BlockSpec((1, 1, 1, 2 * D_FF), wmap),
                      pl.BlockSpec(memory_space=pl.ANY), pl.BlockSpec((1, 1, 1, D_MODEL), wmap)],
            out_specs=row,
            scratch_shapes=[pltpu.VMEM((D_MODEL, 2 * D_FF), F32), pltpu.VMEM((D_FF, D_MODEL), F32),
                            pltpu.VMEM((D_MODEL, 2 * D_FF), BF16), pltpu.VMEM((D_FF, D_MODEL), BF16),
                            pltpu.SemaphoreType.DMA((2,))],
        ),
        out_shape=jax.ShapeDtypeStruct(xs.shape, F32),
        compiler_params=_cparams(("arbitrary",), 54 * MIB),
        name="experts",
    )(tabs["blk_e"], tabs["nb_used"], tabs["next_e"], tabs["blk_valid"], xs, w_gu, b_gu, w_down, b_down)


def _combine_kernel(nch_ref, gst_ref, ntot_ref, x_ref, col_ref, yb_ref, g_ref, b_ref, *rest, dn_alpha, n_prt):
    o_refs, (loc_sc, sems) = rest[:-2], rest[-2:]
    i = pl.program_id(0)
    slot = i % 2

    def fetch(tile, sl):
        loc = loc_sc.at[sl]
        total = _group_chunks(
            nch_ref, gst_ref, tile,
            lambda lc, gc: pltpu.make_async_copy(_chunk(yb_ref, gc), _chunk(loc, lc), sems.at[sl]).start())

        def zero(c, carry):
            _chunk(loc, c)[...] = jnp.zeros((CH, D_MODEL), F32)
            return carry

        lax.fori_loop(total, SLOTS // CH, zero, 0)

    @pl.when(i == 0)
    def _():
        fetch(i, slot)

    @pl.when(i + 1 < pl.num_programs(0))
    def _():
        fetch(i + 1, 1 - slot)

    info = col_ref[0]
    scol = lax.broadcasted_iota(I32, (RT, SLOTS), 1).astype(F32)
    gmat = jnp.zeros((RT, SLOTS), F32)
    for k in range(TOP_K):
        gmat = gmat + jnp.where(scol == info[:, k:k + 1], info[:, TOP_K + k:TOP_K + k + 1], 0.0)

    _wait_chunks(loc_sc.at[slot], yb_ref, sems.at[slot], ntot_ref[i])
    y = _dot(gmat.astype(BF16), loc_sc[slot].astype(BF16))
    out = _layer_norm(dn_alpha * x_ref[...] + y, g_ref[...], b_ref[...])
    if n_prt is None:
        o_refs[0][...] = out
    else:
        @pl.when(i < n_prt)
        def _():
            o_refs[0][...] = out

        @pl.when(i >= n_prt)
        def _():
            o_refs[1][...] = out


def _combine(x1, colinfo, yb, g, b, tabs, dims, dn_alpha, split):
    n = dims.n_tok
    n_prt = dims.n_prompt // RT
    if split:
        out_specs = [pl.BlockSpec((RT, D_MODEL), lambda i, *_: (jnp.minimum(i, n_prt - 1), 0)),
                     pl.BlockSpec((RT, D_MODEL), lambda i, *_: (jnp.maximum(i - n_prt, 0), 0))]
        out_shape = [jax.ShapeDtypeStruct((dims.n_prompt, D_MODEL), F32),
                     jax.ShapeDtypeStruct((dims.n_sample, D_MODEL), F32)]
    else:
        out_specs = pl.BlockSpec((RT, D_MODEL), lambda i, *_: (i, 0))
        out_shape = jax.ShapeDtypeStruct((n, D_MODEL), F32)
    return pl.pallas_call(
        functools.partial(_combine_kernel, dn_alpha=dn_alpha, n_prt=n_prt if split else None),
        grid_spec=pltpu.PrefetchScalarGridSpec(
            num_scalar_prefetch=3,
            grid=(dims.n_rt,),
            in_specs=[pl.BlockSpec((RT, D_MODEL), lambda i, *_: (i, 0)),
                      pl.BlockSpec((1, RT, V7X_LANES), lambda i, *_: (i, 0, 0)),
                      pl.BlockSpec(memory_space=pl.ANY),
                      pl.BlockSpec(g.shape, lambda i, *_: (0, 0)), pl.BlockSpec(b.shape, lambda i, *_: (0, 0))],
            out_specs=out_specs,
            scratch_shapes=[pltpu.VMEM((2, SLOTS, D_MODEL), F32), pltpu.SemaphoreType.DMA((2,))],
        ),
        out_shape=out_shape,
        compiler_params=_cparams(("arbitrary",), 40 * MIB),
        name="combine",
    )(tabs["nch"], tabs["gstart"], tabs["ntot"], x1, colinfo, yb, g, b)


def _swap_halves(w, group):
    shp = w.shape
    w = w.reshape(shp[:-1] + (shp[-1] // group, 2, group // 2))
    return w[..., ::-1, :].reshape(shp)


def _block_diag(blocks, per):
    h, a, b = blocks.shape
    grouped = blocks.reshape(h // per, per, a, b)
    out = jnp.zeros((h // per, per * a, per * b), blocks.dtype)
    for p in range(per):
        out = out.at[:, p * a:(p + 1) * a, p * b:(p + 1) * b].set(grouped[:, p])
    return out


def _rope_tables(dims):
    half = ROPE_DIM // 2
    pos = jnp.concatenate([jnp.arange(dims.seq, dtype=I32),
                           jnp.tile(dims.past + jnp.arange(dims.dec_seq, dtype=I32), dims.dec_batch)]).astype(F32)
    freqs = ROPE_THETA ** (-2.0 * jnp.arange(half, dtype=F32) / ROPE_DIM)
    ang = pos[:, None] * freqs[None, :]
    cos, sin = jnp.cos(ang), jnp.sin(ang)
    cos2 = jnp.concatenate([cos, cos], axis=1)
    sin2 = jnp.concatenate([-sin, sin], axis=1)
    scale = (NOPE_DIM + ROPE_DIM) ** -0.5 * LOG2_E
    n = pos.shape[0]
    pad = jnp.zeros((n, HEAD_SLOT - NOPE_DIM - ROPE_DIM), F32)
    tqa = jnp.concatenate([jnp.full((n, NOPE_DIM), scale, F32), cos2 * scale, pad], axis=1)
    tqb = jnp.concatenate([jnp.zeros((n, NOPE_DIM), F32), sin2 * scale, pad], axis=1)
    return dict(tqa=tqa, tqb=tqb, tabk=jnp.concatenate([cos2, sin2], axis=1))


def _head_slots(nope, rope):
    r = nope.shape[0]
    pad = jnp.zeros((r, MLA_HEADS, HEAD_SLOT - NOPE_DIM - ROPE_DIM), nope.dtype)
    return jnp.concatenate([nope, rope, pad], axis=2).reshape(r, MLA_HEADS * HEAD_SLOT)


def _layer_weights(p, l):
    o4 = 2 * LRU_WIDTH + Q_RANK + KV_RANK
    w_in = p["w_in"][l]
    win = jnp.concatenate([w_in, _swap_halves(w_in[:, o4:], ROPE_DIM)], axis=1).astype(BF16)
    w_uq = p["w_uq"][l].reshape(Q_RANK, MLA_HEADS, NOPE_DIM + ROPE_DIM)
    uq_nope, uq_rope = w_uq[:, :, :NOPE_DIM], w_uq[:, :, NOPE_DIM:]
    w_uk = p["w_uk"][l]
    zero_rope = jnp.zeros((KV_RANK, MLA_HEADS, ROPE_DIM), F32)
    ekr = jnp.zeros((ROPE_DIM, HEAD_SLOT), F32).at[jnp.arange(ROPE_DIM), NOPE_DIM + jnp.arange(ROPE_DIM)].set(1.0)
    wabs = jnp.pad(jnp.transpose(w_uk, (1, 2, 0)), ((0, 0), (0, HEAD_SLOT - NOPE_DIM), (0, 0)))
    return dict(
        win=win,
        wqa=_head_slots(uq_nope, uq_rope).astype(BF16),
        wqb=_head_slots(jnp.zeros_like(uq_nope), _swap_halves(uq_rope, ROPE_DIM)).astype(BF16),
        wkx=_head_slots(w_uk, zero_rope).astype(BF16),
        ekr=ekr.astype(BF16),
        wvt=jnp.pad(jnp.transpose(p["w_uv"][l], (1, 2, 0)), ((0, 0), (0, V_SLOT - V_DIM), (0, 0))
                    ).reshape(MLA_HEADS * V_SLOT, KV_RANK).astype(BF16),
        vone=jnp.tile((jnp.arange(V_SLOT) == V_DIM).astype(F32), MLA_HEADS)[:, None],
        wabs=wabs.astype(BF16),
        wuv=jnp.transpose(p["w_uv"][l], (1, 0, 2)).astype(BF16),
        qg=p["q_norm_g"][l][None, :], kvg=p["kv_norm_g"][l][None, :],
        conv_w=p["conv_w"][l], conv_b=p["conv_b"][l][None, :],
        wrg=_block_diag(p["w_rg"][l], 4).astype(BF16), b_rg=p["b_rg"][l][None, :],
        wig=_block_diag(p["w_ig"][l], 4).astype(BF16), b_ig=p["b_ig"][l][None, :],
        lam=p["lru_lambda"][l][None, :],
        wo=p["w_o"][l].astype(BF16), ln1_g=p["ln1_g"][l][None, :], ln1_b=p["ln1_b"][l][None, :],
        wr_t=p["w_router"][l].T, br=p["b_router"][l][:, None],
        ln2_g=p["ln2_g"][l][None, :], ln2_b=p["ln2_b"][l][None, :],
    )


def _trunk(x_prompt, x_sample, cache_ckv, cache_krope, state_conv, state_lru, p):
    depth = p["w_in"].shape[0]
    dims = Dims(depth, x_prompt.shape[0], x_prompt.shape[1], x_sample.shape[0], x_sample.shape[1],
                cache_ckv.shape[2])
    assert dims.n_sample == TM and dims.seq % TM == 0 and dims.dec_seq == CHUNK and dims.n_tok % RT == 0
    dn_alpha = (2.0 * depth) ** 0.25
    x = (x_prompt.reshape(-1, D_MODEL), x_sample.reshape(-1, D_MODEL))
    rope_tabs = _rope_tables(dims)
    tri = (jnp.arange(RT)[:, None] < jnp.arange(RT)[None, :]).astype(BF16)
    low = (jnp.arange(N_EXPERTS)[None, :] < jnp.arange(N_EXPERTS)[:, None]).astype(BF16)
    b_gu = p["b_gu"][:, :, None, :]
    b_down = p["b_down"][:, :, None, :]
    zero_state = jnp.zeros((dims.batch, V7X_SUBLANES, LRU_WIDTH), F32)
    tps = dims.seq // TM
    n_p = dims.n_prompt
    lo = V7X_SUBLANES - (CONV_W - 1)
    outs = {k: [] for k in ("ckv_p", "kr_p", "conv_p", "h_p", "ckv_s", "kr_s", "conv_s", "h_s")}
    for l in range(depth):
        lw = _layer_weights(p, l)
        ux, ug, c_new, kr_new, qx, kx, vt = _inproj(x, lw, rope_tabs, dims)
        lru_out, st_p = _lru(ux, ug, zero_state, lw, None,
                             seg=TM, n_seg=dims.batch * tps, tps=tps, row0=0)
        lru_out, st_s = _lru(ux, ug, _lru_state(state_conv[l], state_lru[l]), lw, lru_out,
                             seg=dims.dec_seq, n_seg=dims.dec_batch, tps=1, row0=n_p)
        attn = _attn_prompt(qx, kx, vt, dims)
        attn = _attn_sample(qx, cache_ckv, cache_krope, c_new, kr_new, lw, attn, l, dims)
        x1, rowinfo, colinfo, cnt = _outproj(x, lru_out, attn, lw, tri, low, dims, dn_alpha)
        tabs = _routing_tables(cnt[:, :, 0], dims)
        xs = _dispatch(x1, rowinfo, tabs, dims)
        yb = _experts(xs, p["w_gu"], b_gu, p["w_down"], b_down, tabs, l, dims)
        x = _combine(x1, colinfo, yb, lw["ln2_g"], lw["ln2_b"], tabs, dims, dn_alpha, split=l == depth - 1)
        outs["ckv_p"].append(c_new[:n_p].reshape(dims.batch, dims.seq, KV_RANK))
        outs["kr_p"].append(kr_new[:n_p].reshape(dims.batch, dims.seq, ROPE_DIM))
        outs["conv_p"].append(st_p[:, lo:, :])
        outs["h_p"].append(st_p[:, 0, :])
        outs["ckv_s"].append(c_new[n_p:].reshape(dims.dec_batch, dims.dec_seq, KV_RANK))
        outs["kr_s"].append(kr_new[n_p:].reshape(dims.dec_batch, dims.dec_seq, ROPE_DIM))
        outs["conv_s"].append(st_s[:, lo:, :])
        outs["h_s"].append(st_s[:, 0, :])
    st = {k: jnp.stack(v) for k, v in outs.items()}
    y_prompt = x[0].reshape(x_prompt.shape)
    y_sample = x[1].reshape(x_sample.shape)
    return (y_prompt, y_sample, st["ckv_p"], st["kr_p"], st["conv_p"], st["h_p"],
            st["ckv_s"], st["kr_s"], st["conv_s"], st["h_s"])


def kernel(x_prompt, x_sample, cache_ckv, cache_krope, state_conv, state_lru, w_in, conv_w, conv_b, w_rg, b_rg,
           w_ig, b_ig, lru_lambda, q_norm_g, w_uq, kv_norm_g, w_uk, w_uv, w_o, ln1_g, ln1_b, w_router, b_router,
           w_gu, b_gu, w_down, b_down, ln2_g, ln2_b):
    p = dict(w_in=w_in, conv_w=conv_w, conv_b=conv_b, w_rg=w_rg, b_rg=b_rg, w_ig=w_ig, b_ig=b_ig,
             lru_lambda=lru_lambda, q_norm_g=q_norm_g, w_uq=w_uq, kv_norm_g=kv_norm_g, w_uk=w_uk, w_uv=w_uv,
             w_o=w_o, ln1_g=ln1_g, ln1_b=ln1_b, w_router=w_router, b_router=b_router, w_gu=w_gu, b_gu=b_gu,
             w_down=w_down, b_down=b_down, ln2_g=ln2_g, ln2_b=ln2_b)
    return _trunk(x_prompt, x_sample, cache_ckv, cache_krope, state_conv, state_lru, p)
```

```python
import functools
from typing import NamedTuple

import jax
import jax.numpy as jnp
from jax import lax
from jax.experimental import pallas as pl
from jax.experimental.pallas import tpu as pltpu

F32 = jnp.float32
BF16 = jnp.bfloat16
I32 = jnp.int32

D_MODEL = 1024
CHUNK = 64
LRU_WIDTH = 512
CONV_W = 4
LRU_C = 8.0
MLA_HEADS = 8
NOPE_DIM = 64
ROPE_DIM = 32
V_DIM = 64
Q_RANK = 384
KV_RANK = 256
ROPE_THETA = 10000.0
N_EXPERTS = 32
TOP_K = 4
D_FF = 1024
SWIGLU_LIMIT = 7.0
SWIGLU_ALPHA = 1.702
LN_EPS = 1e-5
RMS_EPS = 1e-6
LOG2_E = 1.4426950408889634

V7X_SUBLANES = 8
V7X_LANES = 128
V7X_VMEM_BYTES = 64 * 1024 * 1024
MIB = 1024 * 1024

TM = 512
TQ = 512
TK = 1024
RT = 256
EBLK = 512
ESUB = 128
CH = V7X_SUBLANES
SLOTS = 1280
V_SLOT = 80
HEAD_SLOT = V7X_LANES

assert SLOTS >= RT * TOP_K + N_EXPERTS * (CH - 1) and SLOTS % CH == 0


class Dims(NamedTuple):
    depth: int
    batch: int
    seq: int
    dec_batch: int
    dec_seq: int
    past: int

    @property
    def n_prompt(self):
        return self.batch * self.seq

    @property
    def n_sample(self):
        return self.dec_batch * self.dec_seq

    @property
    def n_tok(self):
        return self.n_prompt + self.n_sample

    @property
    def n_rt(self):
        return self.n_tok // RT

    @property
    def n_eblk(self):
        chunks = self.n_tok * TOP_K // CH + self.n_rt * N_EXPERTS + N_EXPERTS * (EBLK // CH - 1)
        return -(-chunks // (EBLK // CH))


def _cparams(sem, nbytes):
    return pltpu.CompilerParams(dimension_semantics=sem,
                                vmem_limit_bytes=min(nbytes, V7X_VMEM_BYTES - 8 * MIB))


def _dot(a, b):
    return jnp.dot(a, b, preferred_element_type=F32)


def _dot_nt(a, b):
    return lax.dot_general(a, b, (((1,), (1,)), ((), ())), preferred_element_type=F32)


def _layer_norm(v, g, b):
    mu = jnp.mean(v, axis=-1, keepdims=True)
    vc = v - mu
    var = jnp.mean(vc * vc, axis=-1, keepdims=True)
    return vc * lax.rsqrt(var + LN_EPS) * g + b


def _rms_norm(v, g):
    return v * lax.rsqrt(jnp.mean(v * v, axis=-1, keepdims=True) + RMS_EPS) * g


def _inproj_kernel(xp_ref, xs_ref, win_ref, tqa_ref, tqb_ref, tabk_ref, qg_ref, kvg_ref, wqa_ref, wqb_ref, wkx_ref,
                   ekr_ref, wvt_ref, vone_ref, *rest, n_pt):
    ux_ref, ug_ref, cp_ref, cs_ref, krp_ref, krs_ref, qx_ref, kx_ref, vt_ref = rest[-9:]
    is_sample = pl.program_id(0) >= n_pt
    x = jnp.where(is_sample, xs_ref[...], xp_ref[...])
    u = _dot(x.astype(BF16), win_ref[...])
    o1, o2 = LRU_WIDTH, 2 * LRU_WIDTH
    o3 = o2 + Q_RANK
    o4 = o3 + KV_RANK
    ux_ref[...] = u[:, :o1]
    ug_ref[...] = u[:, o1:o2]
    c_new = _rms_norm(u[:, o3:o4], kvg_ref[...])
    cb = c_new.astype(BF16)
    tk = tabk_ref[...]
    kr = (u[:, o4:o4 + ROPE_DIM] * tk[:, :ROPE_DIM]
          + u[:, o4 + ROPE_DIM:o4 + 2 * ROPE_DIM] * tk[:, ROPE_DIM:])

    @pl.when(jnp.logical_not(is_sample))
    def _():
        cp_ref[0] = c_new
        krp_ref[0] = kr

    @pl.when(is_sample)
    def _():
        cs_ref[0] = c_new
        krs_ref[0] = kr

    kn = _dot(cb, wkx_ref[...])
    kr_slot = _dot(kr.astype(BF16), ekr_ref[...])
    vt_ref[...] = (_dot_nt(wvt_ref[...], cb) + vone_ref[...]).astype(BF16)
    qn = _rms_norm(u[:, o2:o3], qg_ref[...]).astype(BF16)
    qa = _dot(qn, wqa_ref[...])
    qb = _dot(qn, wqb_ref[...])
    ta = tqa_ref[...]
    tb = tqb_ref[...]
    for h in range(MLA_HEADS):
        hs = slice(h * HEAD_SLOT, (h + 1) * HEAD_SLOT)
        kx_ref[:, hs] = (kn[:, hs] + kr_slot).astype(BF16)
        qx_ref[:, hs] = (qa[:, hs] * ta + qb[:, hs] * tb).astype(BF16)


def _x_specs(x, n_pt):
    if isinstance(x, tuple):
        return list(x), [pl.BlockSpec((TM, D_MODEL), lambda i: (jnp.minimum(i, n_pt - 1), 0)),
                         pl.BlockSpec((TM, D_MODEL), lambda i: (0, 0))]
    return [x, x], [pl.BlockSpec((TM, D_MODEL), lambda i: (i, 0)), pl.BlockSpec((TM, D_MODEL), lambda i: (n_pt, 0))]


def _inproj(x, lw, tabs, caches, layer, dims):
    n = dims.n_tok
    n_pt = dims.n_prompt // TM
    x_args, x_specs = _x_specs(x, n_pt)
    tps = dims.seq // TM
    tab_map = lambda i: (jnp.where(i < n_pt, i % tps, tps), 0)
    row = lambda w: pl.BlockSpec((TM, w), lambda i: (i, 0))
    full = lambda a: pl.BlockSpec(a.shape, lambda i: (0,) * a.ndim)
    tab = lambda a: pl.BlockSpec((TM, a.shape[1]), tab_map)
    weights = [lw["qg"], lw["kvg"], lw["wqa"], lw["wqb"], lw["wkx"], lw["ekr"], lw["wvt"], lw["vone"]]
    hw = MLA_HEADS * HEAD_SLOT
    args = x_args + [lw["win"], tabs["tqa"], tabs["tqb"], tabs["tabk"]] + weights
    in_specs = (x_specs + [full(lw["win"]), tab(tabs["tqa"]), tab(tabs["tqb"]), tab(tabs["tabk"])]
                + [full(w) for w in weights])
    aliases = {}
    if caches is not None:
        aliases = {len(args) + k: 2 + k for k in range(4)}
        args = args + list(caches)
        in_specs = in_specs + [pl.BlockSpec(memory_space=pl.ANY)] * 4
    p_map = lambda i: (layer, jnp.minimum(i, n_pt - 1), 0)
    s_map = lambda i: (layer, 0, 0)
    return pl.pallas_call(
        functools.partial(_inproj_kernel, n_pt=n_pt),
        grid=(n // TM,),
        in_specs=in_specs,
        out_specs=[row(LRU_WIDTH), row(LRU_WIDTH),
                   pl.BlockSpec((1, TM, KV_RANK), p_map), pl.BlockSpec((1, TM, KV_RANK), s_map),
                   pl.BlockSpec((1, TM, ROPE_DIM), p_map), pl.BlockSpec((1, TM, ROPE_DIM), s_map),
                   row(hw), row(hw), pl.BlockSpec((MLA_HEADS * V_SLOT, TM), lambda i: (0, i))],
        out_shape=[jax.ShapeDtypeStruct((n, LRU_WIDTH), F32), jax.ShapeDtypeStruct((n, LRU_WIDTH), F32),
                   jax.ShapeDtypeStruct((dims.depth, dims.n_prompt, KV_RANK), F32),
                   jax.ShapeDtypeStruct((dims.depth, dims.n_sample, KV_RANK), F32),
                   jax.ShapeDtypeStruct((dims.depth, dims.n_prompt, ROPE_DIM), F32),
                   jax.ShapeDtypeStruct((dims.depth, dims.n_sample, ROPE_DIM), F32),
                   jax.ShapeDtypeStruct((n, hw), BF16), jax.ShapeDtypeStruct((n, hw), BF16),
                   jax.ShapeDtypeStruct((MLA_HEADS * V_SLOT, n), BF16)],
        input_output_aliases=aliases,
        compiler_params=_cparams(("arbitrary",), 48 * MIB),
        name="inproj",
    )(*args)


def _linear_scan(a, b, h0):
    n = a.shape[0]
    sub = lax.broadcasted_iota(I32, a.shape, 0) % V7X_SUBLANES
    d = 1
    while d < V7X_SUBLANES:
        keep = sub >= d
        a_sh = jnp.where(keep, pltpu.roll(a, d, 0), 1.0)
        b_sh = jnp.where(keep, pltpu.roll(b, d, 0), 0.0)
        b = b + a * b_sh
        a = a * a_sh
        d *= 2
    carry = h0
    groups = []
    for g in range(n // V7X_SUBLANES):
        rows = slice(g * V7X_SUBLANES, (g + 1) * V7X_SUBLANES)
        hg = b[rows] + a[rows] * carry
        groups.append(hg)
        carry = hg[V7X_SUBLANES - 1:V7X_SUBLANES]
    return jnp.concatenate(groups, axis=0)


def _sigmoid(v):
    return 0.5 + 0.5 * jnp.tanh(0.5 * v)


def _gelu_tanh(v):
    return 0.5 * v * (1.0 + jnp.tanh(0.7978845608028654 * (v + 0.044715 * v * v * v)))


def _lru_kernel(ux_ref, ug_ref, st0_ref, cw_ref, cb_ref, wrg_ref, brg_ref, wig_ref, big_ref,
                lam_ref, out_ref, st_ref, xs_sc, carry_sc, *, seg, tps):
    t = pl.program_id(0) % tps
    lo = V7X_SUBLANES - (CONV_W - 1)

    @pl.when(t == 0)
    def _():
        carry_sc[...] = st0_ref[0]

    st_ref[0] = jnp.zeros(st_ref.shape[1:], F32)

    xs_sc[lo:V7X_SUBLANES, :] = carry_sc[lo:V7X_SUBLANES, :]
    xs_sc[V7X_SUBLANES:V7X_SUBLANES + seg, :] = ux_ref[...]
    cw = cw_ref[...]
    xc = cb_ref[...] + xs_sc[lo:lo + seg, :] * cw[0:1, :]
    for k in range(1, CONV_W):
        xc = xc + xs_sc[lo + k:lo + k + seg, :] * cw[k:k + 1, :]
    new_conv = xs_sc[seg + lo:seg + V7X_SUBLANES, :]
    carry_sc[lo:V7X_SUBLANES, :] = new_conv
    st_ref[0, lo:V7X_SUBLANES, :] = new_conv

    xcb = xc.astype(BF16)
    half = LRU_WIDTH // 2
    for g in range(2):
        cs = slice(g * half, (g + 1) * half)
        xg = xc[:, cs]
        r = _sigmoid(_dot(xcb[:, cs], wrg_ref[g]) + brg_ref[:, cs])
        ig = _sigmoid(_dot(xcb[:, cs], wig_ref[g]) + big_ref[:, cs])
        nl = -lam_ref[:, cs]
        softplus = jnp.maximum(nl, 0.0) + jnp.log1p(jnp.exp(-jnp.abs(nl)))
        log_a = -LRU_C * r * softplus
        a = jnp.exp(log_a)
        gap = 1.0 - a * a
        bt = jnp.where(gap > 0.0, gap * lax.rsqrt(gap), 0.0) * ig * xg
        h = _linear_scan(a, bt, carry_sc[0:1, cs])
        h_last = h[seg - 1:seg, :]
        carry_sc[0:1, cs] = h_last
        st_ref[0, 0:1, cs] = h_last
        out_ref[:, cs] = (_gelu_tanh(ug_ref[:, cs]) * h).astype(BF16)


def _lru_state(conv, h):
    lo = V7X_SUBLANES - (CONV_W - 1)
    return jnp.concatenate([h[:, None, :], jnp.zeros((h.shape[0], lo - 1, h.shape[1]), F32), conv], axis=1)


def _lru(ux, ug, st0, lw, prev_out, *, seg, n_seg, tps, row0):
    n = ux.shape[0]
    blk0 = row0 // seg
    row = pl.BlockSpec((seg, LRU_WIDTH), lambda i: (blk0 + i, 0))
    full = lambda a: pl.BlockSpec(a.shape, lambda i: (0,) * a.ndim)
    n_seq = n_seg // tps
    weights = [lw["conv_w"], lw["conv_b"], lw["wrg"], lw["b_rg"], lw["wig"], lw["b_ig"], lw["lam"]]
    in_specs = [row, row,
                pl.BlockSpec((1, V7X_SUBLANES, LRU_WIDTH), lambda i: (i // tps, 0, 0))] + [full(w) for w in weights]
    args = [ux, ug, st0] + weights
    aliases = {}
    if prev_out is not None:
        in_specs.append(pl.BlockSpec(memory_space=pl.ANY))
        args.append(prev_out)
        aliases = {len(args) - 1: 0}

    def body(*refs):
        refs = list(refs)
        if prev_out is not None:
            del refs[len(args) - 1]
        _lru_kernel(*refs, seg=seg, tps=tps)

    return pl.pallas_call(
        body,
        grid=(n_seg,),
        in_specs=in_specs,
        out_specs=[row, pl.BlockSpec((1, V7X_SUBLANES, LRU_WIDTH), lambda i: (i // tps, 0, 0))],
        out_shape=[jax.ShapeDtypeStruct((n, LRU_WIDTH), BF16),
                   jax.ShapeDtypeStruct((n_seq, V7X_SUBLANES, LRU_WIDTH), F32)],
        scratch_shapes=[pltpu.VMEM((seg + V7X_SUBLANES, LRU_WIDTH), F32),
                        pltpu.VMEM((V7X_SUBLANES, LRU_WIDTH), F32)],
        input_output_aliases=aliases,
        compiler_params=_cparams(("arbitrary",), 40 * MIB),
        name="lru_seg%d" % seg,
    )(*args)


def _attn_prompt_kernel(qi_ref, kj_ref, mk_ref, qx_ref, kx_ref, vt_ref, o_ref, m_sc, acc_sc):
    pair = pl.program_id(1)
    i = qi_ref[pair]
    j = kj_ref[pair]
    flag = mk_ref[pair]

    @pl.when(j == 0)
    def _():
        m_sc[...] = jnp.full(m_sc.shape, -jnp.inf, F32)
        acc_sc[...] = jnp.zeros(acc_sc.shape, F32)

    def step(masked):
        if masked:
            kpos = lax.broadcasted_iota(I32, (TK, TQ), 0) + j * TK
            qpos = lax.broadcasted_iota(I32, (TK, TQ), 1) + i * TQ
            visible = kpos < (qpos // CHUNK + 1) * CHUNK

        def scores(h):
            hs = slice(h * HEAD_SLOT, (h + 1) * HEAD_SLOT)
            return _dot_nt(kx_ref[:, hs], qx_ref[:, hs])

        pending = [scores(0), scores(1)]
        for h in range(MLA_HEADS):
            s = pending.pop(0)
            if h + 2 < MLA_HEADS:
                pending.append(scores(h + 2))
            if masked:
                s = jnp.where(visible, s, -jnp.inf)
            m_prev = m_sc[h]
            m_new = jnp.maximum(m_prev, jnp.max(s, axis=0, keepdims=True))
            alpha = jnp.exp2(m_prev - m_new)
            p = jnp.exp2(s - m_new).astype(BF16)
            acc_sc[h] = alpha * acc_sc[h] + _dot(vt_ref[h * V_SLOT:(h + 1) * V_SLOT, :], p)
            m_sc[h] = m_new

    @pl.when((flag & 1) == 0)
    def _():
        step(False)

    @pl.when((flag & 1) == 1)
    def _():
        step(True)

    @pl.when((flag & 2) == 2)
    def _():
        o_t = jnp.concatenate([acc_sc[h, :V_DIM, :] / acc_sc[h, V_DIM:V_DIM + 1, :] for h in range(MLA_HEADS)],
                              axis=0)
        o_ref[...] = o_t.T.astype(BF16)


def _attn_prompt(qx, kx, vt, dims):
    n = dims.n_tok
    nq = dims.seq // TQ
    nk = dims.seq // TK
    pairs = []
    for i in range(nq):
        n_kv = -(-(i + 1) * TQ // TK)
        for j in range(n_kv):
            pairs.append((i, j, int((j + 1) * TK > i * TQ) | (2 if j == n_kv - 1 else 0)))
    qi = jnp.asarray([p[0] for p in pairs], I32)
    kj = jnp.asarray([p[1] for p in pairs], I32)
    mk = jnp.asarray([p[2] for p in pairs], I32)
    hw = MLA_HEADS * HEAD_SLOT
    return pl.pallas_call(
        _attn_prompt_kernel,
        grid_spec=pltpu.PrefetchScalarGridSpec(
            num_scalar_prefetch=3,
            grid=(dims.batch, len(pairs)),
            in_specs=[pl.BlockSpec((TQ, hw), lambda b, p, qi, kj, mk: (b * nq + qi[p], 0)),
                      pl.BlockSpec((TK, hw), lambda b, p, qi, kj, mk: (b * nk + kj[p], 0)),
                      pl.BlockSpec((MLA_HEADS * V_SLOT, TK), lambda b, p, qi, kj, mk: (0, b * nk + kj[p]))],
            out_specs=pl.BlockSpec((TQ, MLA_HEADS * V_DIM), lambda b, p, qi, kj, mk: (b * nq + qi[p], 0)),
            scratch_shapes=[pltpu.VMEM((MLA_HEADS, 1, TQ), F32), pltpu.VMEM((MLA_HEADS, V_SLOT, TQ), F32)],
        ),
        out_shape=jax.ShapeDtypeStruct((n, MLA_HEADS * V_DIM), BF16),
        compiler_params=_cparams(("parallel", "arbitrary"), 48 * MIB),
        name="attn_prompt",
    )(qi, kj, mk, qx, kx, vt)


def _attn_sample_kernel(qx_ref, pk_ref, pkr_ref, c_ref, kr_ref, wabs_ref, ekr_ref, wuv_ref, prev_ref, o_ref,
                        *, dec_seq):
    del prev_ref
    slots = [qx_ref[:, h * HEAD_SLOT:(h + 1) * HEAD_SLOT] for h in range(MLA_HEADS)]
    q_slot = jnp.concatenate(slots, axis=0)
    q_lat = jnp.concatenate([_dot(slots[h], wabs_ref[h]) for h in range(MLA_HEADS)], axis=0).astype(BF16)
    ekr = ekr_ref[...]
    pk = pk_ref[0, 0].astype(BF16)
    pkr = _dot(pkr_ref[0, 0].astype(BF16), ekr).astype(BF16)
    k = c_ref[0].astype(BF16)
    kr = _dot(kr_ref[0].astype(BF16), ekr).astype(BF16)
    s_past = _dot_nt(q_lat, pk) + _dot_nt(q_slot, pkr)
    s_new = _dot_nt(q_lat, k) + _dot_nt(q_slot, kr)
    m = jnp.maximum(jnp.max(s_past, axis=1, keepdims=True), jnp.max(s_new, axis=1, keepdims=True))
    p_past = jnp.exp2(s_past - m)
    p_new = jnp.exp2(s_new - m)
    denom = jnp.sum(p_past, axis=1, keepdims=True) + jnp.sum(p_new, axis=1, keepdims=True)
    o_lat = (_dot(p_past.astype(BF16), pk) + _dot(p_new.astype(BF16), k)) / denom
    for h in range(MLA_HEADS):
        o_h = o_lat[h * dec_seq:(h + 1) * dec_seq, :].astype(BF16)
        o_ref[:, h * V_DIM:(h + 1) * V_DIM] = _dot(o_h, wuv_ref[h]).astype(BF16)


def _attn_sample(qx, cache_ckv, cache_krope, c_s, kr_s, lw, attn_prev, layer, dims):
    ds = dims.dec_seq
    blk0 = dims.n_prompt // ds
    rmap = lambda s: (blk0 + s, 0)
    smap = lambda s: (layer, s, 0)
    past = cache_ckv.shape[2]
    full = lambda a: pl.BlockSpec(a.shape, lambda s: (0,) * a.ndim)
    return pl.pallas_call(
        functools.partial(_attn_sample_kernel, dec_seq=ds),
        grid=(dims.dec_batch,),
        in_specs=[pl.BlockSpec((ds, MLA_HEADS * HEAD_SLOT), rmap),
                  pl.BlockSpec((1, 1, past, KV_RANK), lambda s: (layer, s, 0, 0)),
                  pl.BlockSpec((1, 1, past, ROPE_DIM), lambda s: (layer, s, 0, 0)),
                  pl.BlockSpec((1, ds, KV_RANK), smap), pl.BlockSpec((1, ds, ROPE_DIM), smap),
                  full(lw["wabs"]), full(lw["ekr"]), full(lw["wuv"]),
                  pl.BlockSpec(memory_space=pl.ANY)],
        out_specs=pl.BlockSpec((ds, MLA_HEADS * V_DIM), rmap),
        out_shape=jax.ShapeDtypeStruct(attn_prev.shape, attn_prev.dtype),
        input_output_aliases={8: 0},
        compiler_params=_cparams(("parallel",), 48 * MIB),
        name="attn_sample",
    )(qx, cache_ckv, cache_krope, c_s, kr_s, lw["wabs"], lw["ekr"], lw["wuv"], attn_prev)


def _outproj_kernel(xp_ref, xs_ref, lru_ref, attn_ref, wo_ref, g_ref, b_ref, wr_ref, br_ref, tri_ref, low_ref,
                    o_ref, row_ref, col_ref, cnt_ref, *, dn_alpha, n_pt):
    x = jnp.where(pl.program_id(0) >= n_pt, xs_ref[...], xp_ref[...])
    mix = _dot(lru_ref[...], wo_ref[:LRU_WIDTH, :]) + _dot(attn_ref[...], wo_ref[LRU_WIDTH:, :])
    x1 = _layer_norm(dn_alpha * x + mix, g_ref[...], b_ref[...])
    o_ref[...] = x1
    for r in range(TM // RT):
        info, cnt = _route_tile(x1[r * RT:(r + 1) * RT, :], wr_ref[...], br_ref[...], tri_ref[...], low_ref[...])
        row_ref[r] = info
        pad = jnp.zeros((V7X_LANES - 2 * TOP_K, RT), F32)
        col_ref[r] = jnp.concatenate([info, pad], axis=0).T
        cnt_ref[r] = jnp.broadcast_to(cnt, cnt_ref.shape[1:])


def _outproj(x, lru_out, attn, lw, tri, low, dims, dn_alpha):
    n = dims.n_tok
    n_pt = dims.n_prompt // TM
    per = TM // RT
    x_args, x_specs = _x_specs(x, n_pt)
    row = lambda w: pl.BlockSpec((TM, w), lambda i: (i, 0))
    full = lambda a: pl.BlockSpec(a.shape, lambda i: (0,) * a.ndim)
    consts = [lw["wo"], lw["ln1_g"], lw["ln1_b"], lw["wr_t"], lw["br"], tri, low]
    return pl.pallas_call(
        functools.partial(_outproj_kernel, dn_alpha=dn_alpha, n_pt=n_pt),
        grid=(n // TM,),
        in_specs=x_specs + [row(LRU_WIDTH), row(MLA_HEADS * V_DIM)] + [full(c) for c in consts],
        out_specs=[row(D_MODEL),
                   pl.BlockSpec((per, 2 * TOP_K, RT), lambda i: (i, 0, 0)),
                   pl.BlockSpec((per, RT, V7X_LANES), lambda i: (i, 0, 0)),
                   pl.BlockSpec((per, N_EXPERTS, V7X_LANES), lambda i: (i, 0, 0))],
        out_shape=[jax.ShapeDtypeStruct((n, D_MODEL), F32),
                   jax.ShapeDtypeStruct((dims.n_rt, 2 * TOP_K, RT), F32),
                   jax.ShapeDtypeStruct((dims.n_rt, RT, V7X_LANES), F32),
                   jax.ShapeDtypeStruct((dims.n_rt, N_EXPERTS, V7X_LANES), F32)],
        compiler_params=_cparams(("parallel",), 40 * MIB),
        name="outproj",
    )(*x_args, lru_out, attn, *consts)


def _split_bf16(v):
    hi = v.astype(BF16)
    return hi, (v - hi.astype(F32)).astype(BF16)


def _route_tile(x, wr, br, tri, low):
    x_hi, x_lo = _split_bf16(x)
    w_hi, w_lo = _split_bf16(wr)
    logits = _dot_nt(w_hi, x_hi) + (_dot_nt(w_hi, x_lo) + _dot_nt(w_lo, x_hi)) + br
    eidx = lax.broadcasted_iota(I32, logits.shape, 0)
    work = logits
    onehots, vals = [], []
    for _ in range(TOP_K):
        m = jnp.max(work, axis=0, keepdims=True)
        first = jnp.min(jnp.where(work == m, eidx, N_EXPERTS), axis=0, keepdims=True)
        oh = eidx == first
        onehots.append(oh)
        vals.append(m)
        work = jnp.where(oh, -jnp.inf, work)
    ex = [jnp.exp(v - vals[0]) for v in vals]
    denom = ex[0] + ex[1] + ex[2] + ex[3]
    sel = onehots[0] | onehots[1] | onehots[2] | onehots[3]
    self32 = sel.astype(F32)
    rank = _dot(self32.astype(BF16), tri)
    cnt = jnp.sum(self32, axis=1, keepdims=True)
    padded = jnp.floor((cnt + (CH - 1)) * (1.0 / CH)) * CH
    lstart = _dot(low, jnp.broadcast_to(padded, (N_EXPERTS, V7X_LANES)).astype(BF16))[:, 0:1]
    slotmat = lstart + rank
    rows = [jnp.sum(jnp.where(oh, slotmat, 0.0), axis=0, keepdims=True) for oh in onehots]
    rows += [e / denom for e in ex]
    return jnp.concatenate(rows, axis=0), cnt


def _routing_tables(cnt, dims):
    cpb = EBLK // CH
    nch = (cnt.astype(I32) + (CH - 1)) // CH
    tot = jnp.sum(nch, axis=0)
    nblk = (tot + cpb - 1) // cpb
    blk_end = jnp.cumsum(nblk)
    base = (blk_end - nblk) * cpb
    gstart = base[None, :] + jnp.cumsum(nch, axis=0) - nch
    blk_ids = jnp.arange(dims.n_eblk, dtype=I32)
    blk_e = jnp.minimum(jnp.sum((blk_end[None, :] <= blk_ids[:, None]).astype(I32), axis=1), N_EXPERTS - 1)
    e_ids = jnp.arange(N_EXPERTS, dtype=I32)
    owner = (blk_e[:, None] == e_ids[None, :]).astype(I32)
    first_blk = jnp.sum(owner * (blk_end - nblk)[None, :], axis=1)
    blk_valid = jnp.clip(jnp.sum(owner * tot[None, :], axis=1) * CH - (blk_ids - first_blk) * EBLK,
                         0, EBLK).astype(I32)
    later_used = (e_ids[None, :] > e_ids[:, None]) & (nblk[None, :] > 0)
    next_e = jnp.min(jnp.where(later_used, e_ids[None, :], N_EXPERTS), axis=1)
    next_e = jnp.where(next_e < N_EXPERTS, next_e, -1).astype(I32)
    return dict(nch=nch.reshape(-1), gstart=gstart.astype(I32).reshape(-1), ntot=jnp.sum(nch, axis=1).astype(I32),
                next_e=next_e, blk_valid=blk_valid,
                tail_start=(base + tot).astype(I32), tail_n=(nblk * cpb - tot).astype(I32),
                blk_e=blk_e, nb_used=blk_end[-1:].astype(I32))


def _group_chunks(nch_ref, gst_ref, tile, visit):
    l0 = jnp.int32(0)
    for e in range(N_EXPERTS):
        n = nch_ref[tile * N_EXPERTS + e]
        g0 = gst_ref[tile * N_EXPERTS + e]

        def body(c, carry, l0=l0, g0=g0):
            visit(l0 + c, g0 + c)
            return carry

        lax.fori_loop(0, n, body, 0)
        l0 = l0 + n
    return l0


def _chunk(ref, c):
    if isinstance(c, int):
        return ref.at[pl.ds(c * CH, CH)]
    return ref.at[pl.ds(pl.multiple_of(c * CH, CH), CH)]


def _wait_chunks(vmem_ref, hbm_ref, sem, count):
    @pl.when(count > 0)
    def _():
        rows = pl.ds(0, count * CH)
        pltpu.make_async_copy(vmem_ref.at[rows], hbm_ref.at[rows], sem).wait()


def _dispatch_kernel(nch_ref, gst_ref, ntot_ref, tls_ref, tln_ref, x_ref, row_ref, xs_ref, loc_sc, zero_sc, sems):
    i = pl.program_id(0)
    last = pl.num_programs(0) - 1
    slot = i % 2

    @pl.when(i >= 2)
    def _():
        _wait_chunks(loc_sc.at[slot], xs_ref, sems.at[slot], ntot_ref[jnp.maximum(i - 2, 0)])

    slots = row_ref[0]
    srow = lax.broadcasted_iota(I32, (SLOTS, RT), 0).astype(F32)
    perm = jnp.zeros((SLOTS, RT), F32)
    for k in range(TOP_K):
        perm = jnp.where(srow == slots[k:k + 1], 1.0, perm)
    loc_sc[slot] = _dot(perm.astype(BF16), x_ref[...].astype(BF16))
    loc = loc_sc.at[slot]
    _group_chunks(nch_ref, gst_ref, i,
                  lambda lc, gc: pltpu.make_async_copy(_chunk(loc, lc), _chunk(xs_ref, gc), sems.at[slot]).start())

    @pl.when(i == last)
    def _():
        _wait_chunks(loc, xs_ref, sems.at[slot], ntot_ref[i])

        @pl.when(i >= 1)
        def _():
            _wait_chunks(loc_sc.at[1 - slot], xs_ref, sems.at[1 - slot], ntot_ref[jnp.maximum(i - 1, 0)])

        zero_sc[...] = jnp.zeros(zero_sc.shape, F32)
        n_tail = jnp.int32(0)
        for e in range(N_EXPERTS):
            n = tln_ref[e]
            g0 = tls_ref[e]

            def body(c, carry, g0=g0):
                pltpu.make_async_copy(zero_sc, _chunk(xs_ref, g0 + c), sems.at[0]).start()
                return carry

            lax.fori_loop(0, n, body, 0)
            n_tail = n_tail + n

        def wait_tail(c, carry):
            pltpu.make_async_copy(zero_sc, _chunk(xs_ref, 0), sems.at[0]).wait()
            return carry

        lax.fori_loop(0, n_tail, wait_tail, 0)


def _dispatch(x1, rowinfo, tabs, dims):
    rows = dims.n_eblk * EBLK
    return pl.pallas_call(
        _dispatch_kernel,
        grid_spec=pltpu.PrefetchScalarGridSpec(
            num_scalar_prefetch=5,
            grid=(dims.n_rt,),
            in_specs=[pl.BlockSpec((RT, D_MODEL), lambda i, *_: (i, 0)),
                      pl.BlockSpec((1, 2 * TOP_K, RT), lambda i, *_: (i, 0, 0))],
            out_specs=pl.BlockSpec(memory_space=pl.ANY),
            scratch_shapes=[pltpu.VMEM((2, SLOTS, D_MODEL), F32), pltpu.VMEM((CH, D_MODEL), F32),
                            pltpu.SemaphoreType.DMA((2,))],
        ),
        out_shape=jax.ShapeDtypeStruct((rows, D_MODEL), F32),
        compiler_params=_cparams(("arbitrary",), 40 * MIB),
        name="dispatch",
    )(tabs["nch"], tabs["gstart"], tabs["ntot"], tabs["tail_start"], tabs["tail_n"], x1, rowinfo)


def _expert_kernel(be_ref, nb_ref, nxt_ref, bv_ref, xs_ref, wgu_hbm, bgu_ref, wd_hbm, bd_ref, y_ref,
                   wgu_st, wd_st, wgu_sc, wd_sc, sems, *, layer):
    b = pl.program_id(0)

    def copies(e):
        return (pltpu.make_async_copy(wgu_hbm.at[layer, e], wgu_st, sems.at[0]),
                pltpu.make_async_copy(wd_hbm.at[layer, e], wd_st, sems.at[1]))

    def mlp(n_rows):
        rows = slice(0, n_rows)
        h = _dot(xs_ref[rows, :].astype(BF16), wgu_sc[...]) + bgu_ref[0, 0]
        hg = jnp.minimum(h[:, :D_FF], SWIGLU_LIMIT)
        hl = jnp.clip(h[:, D_FF:], -SWIGLU_LIMIT, SWIGLU_LIMIT)
        act = hg * jax.nn.sigmoid(SWIGLU_ALPHA * hg) * (hl + 1.0)
        y_ref[rows, :] = _dot(act.astype(BF16), wd_sc[...]) + bd_ref[0, 0]

    @pl.when(b < nb_ref[0])
    def _():
        e = be_ref[b]

        @pl.when(b == 0)
        def _():
            for c in copies(e):
                c.start()

        @pl.when((b == 0) | (e != be_ref[jnp.maximum(b - 1, 0)]))
        def _():
            for c in copies(e):
                c.wait()
            wgu_sc[...] = wgu_st[...].astype(BF16)
            wd_sc[...] = wd_st[...].astype(BF16)
            nxt = nxt_ref[e]

            @pl.when(nxt >= 0)
            def _():
                for c in copies(nxt):
                    c.start()

        valid = bv_ref[b]
        for n_rows in range(ESUB, EBLK + 1, ESUB):
            @pl.when((valid > n_rows - ESUB) & (valid <= n_rows))
            def _(n_rows=n_rows):
                mlp(n_rows)


def _experts(xs, w_gu, b_gu, w_down, b_down, tabs, layer, dims):
    blk = lambda b, be, nb, *_: jnp.minimum(b, nb[0] - 1)
    wmap = lambda b, be, nb, *_: (layer, be[blk(b, be, nb)], 0, 0)
    row = pl.BlockSpec((EBLK, D_MODEL), lambda b, be, nb, *_: (blk(b, be, nb), 0))
    return pl.pallas_call(
        functools.partial(_expert_kernel, layer=layer),
        grid_spec=pltpu.PrefetchScalarGridSpec(
            num_scalar_prefetch=4,
            grid=(dims.n_eblk,),
            in_specs=[row, pl.BlockSpec(memory_space=pl.ANY), pl.BlockSpec((1, 1, 1, 2 * D_FF), wmap),
                      pl.BlockSpec(memory_space=pl.ANY), pl.BlockSpec((1, 1, 1, D_MODEL), wmap)],
            out_specs=row,
            scratch_shapes=[pltpu.VMEM((D_MODEL, 2 * D_FF), F32), pltpu.VMEM((D_FF, D_MODEL), F32),
                            pltpu.VMEM((D_MODEL, 2 * D_FF), BF16), pltpu.VMEM((D_FF, D_MODEL), BF16),
                            pltpu.SemaphoreType.DMA((2,))],
        ),
        out_shape=jax.ShapeDtypeStruct(xs.shape, F32),
        compiler_params=_cparams(("arbitrary",), 54 * MIB),
        name="experts",
    )(tabs["blk_e"], tabs["nb_used"], tabs["next_e"], tabs["blk_valid"], xs, w_gu, b_gu, w_down, b_down)


def _combine_kernel(nch_ref, gst_ref, ntot_ref, x_ref, col_ref, yb_ref, g_ref, b_ref, *rest, dn_alpha, n_prt):
    o_refs, (loc_sc, sems) = rest[:-2], rest[-2:]
    i = pl.program_id(0)
    slot = i % 2

    def fetch(tile, sl):
        loc = loc_sc.at[sl]
        total = _group_chunks(
            nch_ref, gst_ref, tile,
            lambda lc, gc: pltpu.make_async_copy(_chunk(yb_ref, gc), _chunk(loc, lc), sems.at[sl]).start())

        def zero(c, carry):
            _chunk(loc, c)[...] = jnp.zeros((CH, D_MODEL), F32)
            return carry

        lax.fori_loop(total, SLOTS // CH, zero, 0)

    @pl.when(i == 0)
    def _():
        fetch(i, slot)

    @pl.when(i + 1 < pl.num_programs(0))
    def _():
        fetch(i + 1, 1 - slot)

    info = col_ref[0]
    scol = lax.broadcasted_iota(I32, (RT, SLOTS), 1).astype(F32)
    gmat = jnp.zeros((RT, SLOTS), F32)
    for k in reversed(range(TOP_K)):
        gmat = jnp.where(scol == info[:, k:k + 1], info[:, TOP_K + k:TOP_K + k + 1], gmat)

    _wait_chunks(loc_sc.at[slot], yb_ref, sems.at[slot], ntot_ref[i])
    y = _dot(gmat.astype(BF16), loc_sc[slot].astype(BF16))
    out = _layer_norm(dn_alpha * x_ref[...] + y, g_ref[...], b_ref[...])
    if n_prt is None:
        o_refs[0][...] = out
    else:
        @pl.when(i < n_prt)
        def _():
            o_refs[0][...] = out

        @pl.when(i >= n_prt)
        def _():
            o_refs[1][...] = out


def _combine(x1, colinfo, yb, g, b, tabs, dims, dn_alpha, split):
    n = dims.n_tok
    n_prt = dims.n_prompt // RT
    if split:
        out_specs = [pl.BlockSpec((RT, D_MODEL), lambda i, *_: (jnp.minimum(i, n_prt - 1), 0)),
                     pl.BlockSpec((RT, D_MODEL), lambda i, *_: (jnp.maximum(i - n_prt, 0), 0))]
        out_shape = [jax.ShapeDtypeStruct((dims.n_prompt, D_MODEL), F32),
                     jax.ShapeDtypeStruct((dims.n_sample, D_MODEL), F32)]
    else:
        out_specs = pl.BlockSpec((RT, D_MODEL), lambda i, *_: (i, 0))
        out_shape = jax.ShapeDtypeStruct((n, D_MODEL), F32)
    return pl.pallas_call(
        functools.partial(_combine_kernel, dn_alpha=dn_alpha, n_prt=n_prt if split else None),
        grid_spec=pltpu.PrefetchScalarGridSpec(
            num_scalar_prefetch=3,
            grid=(dims.n_rt,),
            in_specs=[pl.BlockSpec((RT, D_MODEL), lambda i, *_: (i, 0)),
                      pl.BlockSpec((1, RT, V7X_LANES), lambda i, *_: (i, 0, 0)),
                      pl.BlockSpec(memory_space=pl.ANY),
                      pl.BlockSpec(g.shape, lambda i, *_: (0, 0)), pl.BlockSpec(b.shape, lambda i, *_: (0, 0))],
            out_specs=out_specs,
            scratch_shapes=[pltpu.VMEM((2, SLOTS, D_MODEL), F32), pltpu.SemaphoreType.DMA((2,))],
        ),
        out_shape=out_shape,
        compiler_params=_cparams(("arbitrary",), 40 * MIB),
        name="combine",
    )(tabs["nch"], tabs["gstart"], tabs["ntot"], x1, colinfo, yb, g, b)


def _swap_halves(w, group):
    shp = w.shape
    w = w.reshape(shp[:-1] + (shp[-1] // group, 2, group // 2))
    return w[..., ::-1, :].reshape(shp)


def _block_diag(blocks, per):
    h, a, b = blocks.shape
    grouped = blocks.reshape(h // per, per, a, b)
    out = jnp.zeros((h // per, per * a, per * b), blocks.dtype)
    for p in range(per):
        out = out.at[:, p * a:(p + 1) * a, p * b:(p + 1) * b].set(grouped[:, p])
    return out


def _rope_tables(dims):
    half = ROPE_DIM // 2
    pos = jnp.concatenate([jnp.arange(dims.seq, dtype=I32),
                           jnp.tile(dims.past + jnp.arange(dims.dec_seq, dtype=I32), dims.dec_batch)]).astype(F32)
    freqs = ROPE_THETA ** (-2.0 * jnp.arange(half, dtype=F32) / ROPE_DIM)
    ang = pos[:, None] * freqs[None, :]
    cos, sin = jnp.cos(ang), jnp.sin(ang)
    cos2 = jnp.concatenate([cos, cos], axis=1)
    sin2 = jnp.concatenate([-sin, sin], axis=1)
    scale = (NOPE_DIM + ROPE_DIM) ** -0.5 * LOG2_E
    n = pos.shape[0]
    pad = jnp.zeros((n, HEAD_SLOT - NOPE_DIM - ROPE_DIM), F32)
    tqa = jnp.concatenate([jnp.full((n, NOPE_DIM), scale, F32), cos2 * scale, pad], axis=1)
    tqb = jnp.concatenate([jnp.zeros((n, NOPE_DIM), F32), sin2 * scale, pad], axis=1)
    return dict(tqa=tqa, tqb=tqb, tabk=jnp.concatenate([cos2, sin2], axis=1))


def _head_slots(nope, rope):
    r = nope.shape[0]
    pad = jnp.zeros((r, MLA_HEADS, HEAD_SLOT - NOPE_DIM - ROPE_DIM), nope.dtype)
    return jnp.concatenate([nope, rope, pad], axis=2).reshape(r, MLA_HEADS * HEAD_SLOT)


def _layer_weights(p, l):
    o4 = 2 * LRU_WIDTH + Q_RANK + KV_RANK
    w_in = p["w_in"][l]
    win = jnp.concatenate([w_in, _swap_halves(w_in[:, o4:], ROPE_DIM)], axis=1).astype(BF16)
    w_uq = p["w_uq"][l].reshape(Q_RANK, MLA_HEADS, NOPE_DIM + ROPE_DIM)
    uq_nope, uq_rope = w_uq[:, :, :NOPE_DIM], w_uq[:, :, NOPE_DIM:]
    w_uk = p["w_uk"][l]
    zero_rope = jnp.zeros((KV_RANK, MLA_HEADS, ROPE_DIM), F32)
    ekr = jnp.zeros((ROPE_DIM, HEAD_SLOT), F32).at[jnp.arange(ROPE_DIM), NOPE_DIM + jnp.arange(ROPE_DIM)].set(1.0)
    wabs = jnp.pad(jnp.transpose(w_uk, (1, 2, 0)), ((0, 0), (0, HEAD_SLOT - NOPE_DIM), (0, 0)))
    return dict(
        win=win,
        wqa=_head_slots(uq_nope, uq_rope).astype(BF16),
        wqb=_head_slots(jnp.zeros_like(uq_nope), _swap_halves(uq_rope, ROPE_DIM)).astype(BF16),
        wkx=_head_slots(w_uk, zero_rope).astype(BF16),
        ekr=ekr.astype(BF16),
        wvt=jnp.pad(jnp.transpose(p["w_uv"][l], (1, 2, 0)), ((0, 0), (0, V_SLOT - V_DIM), (0, 0))
                    ).reshape(MLA_HEADS * V_SLOT, KV_RANK).astype(BF16),
        vone=jnp.tile((jnp.arange(V_SLOT) == V_DIM).astype(F32), MLA_HEADS)[:, None],
        wabs=wabs.astype(BF16),
        wuv=jnp.transpose(p["w_uv"][l], (1, 0, 2)).astype(BF16),
        qg=p["q_norm_g"][l][None, :], kvg=p["kv_norm_g"][l][None, :],
        conv_w=p["conv_w"][l], conv_b=p["conv_b"][l][None, :],
        wrg=_block_diag(p["w_rg"][l], 4).astype(BF16), b_rg=p["b_rg"][l][None, :],
        wig=_block_diag(p["w_ig"][l], 4).astype(BF16), b_ig=p["b_ig"][l][None, :],
        lam=p["lru_lambda"][l][None, :],
        wo=p["w_o"][l].astype(BF16), ln1_g=p["ln1_g"][l][None, :], ln1_b=p["ln1_b"][l][None, :],
        wr_t=p["w_router"][l].T, br=p["b_router"][l][:, None],
        ln2_g=p["ln2_g"][l][None, :], ln2_b=p["ln2_b"][l][None, :],
    )


def _trunk(x_prompt, x_sample, cache_ckv, cache_krope, state_conv, state_lru, p):
    depth = p["w_in"].shape[0]
    dims = Dims(depth, x_prompt.shape[0], x_prompt.shape[1], x_sample.shape[0], x_sample.shape[1],
                cache_ckv.shape[2])
    assert dims.n_sample == TM and dims.seq % TM == 0 and dims.dec_seq == CHUNK and dims.n_tok % RT == 0
    dn_alpha = (2.0 * depth) ** 0.25
    x = (x_prompt.reshape(-1, D_MODEL), x_sample.reshape(-1, D_MODEL))
    rope_tabs = _rope_tables(dims)
    tri = (jnp.arange(RT)[:, None] < jnp.arange(RT)[None, :]).astype(BF16)
    low = (jnp.arange(N_EXPERTS)[None, :] < jnp.arange(N_EXPERTS)[:, None]).astype(BF16)
    b_gu = p["b_gu"][:, :, None, :]
    b_down = p["b_down"][:, :, None, :]
    zero_state = jnp.zeros((dims.batch, V7X_SUBLANES, LRU_WIDTH), F32)
    tps = dims.seq // TM
    n_p = dims.n_prompt
    lo = V7X_SUBLANES - (CONV_W - 1)
    outs = {k: [] for k in ("conv_p", "h_p", "conv_s", "h_s")}
    caches = None
    for l in range(depth):
        lw = _layer_weights(p, l)
        ux, ug, *caches, qx, kx, vt = _inproj(x, lw, rope_tabs, caches, l, dims)
        lru_out, st_p = _lru(ux, ug, zero_state, lw, None,
                             seg=TM, n_seg=dims.batch * tps, tps=tps, row0=0)
        lru_out, st_s = _lru(ux, ug, _lru_state(state_conv[l], state_lru[l]), lw, lru_out,
                             seg=dims.dec_seq, n_seg=dims.dec_batch, tps=1, row0=n_p)
        attn = _attn_prompt(qx, kx, vt, dims)
        attn = _attn_sample(qx, cache_ckv, cache_krope, caches[1], caches[3], lw, attn, l, dims)
        x1, rowinfo, colinfo, cnt = _outproj(x, lru_out, attn, lw, tri, low, dims, dn_alpha)
        tabs = _routing_tables(cnt[:, :, 0], dims)
        xs = _dispatch(x1, rowinfo, tabs, dims)
        yb = _experts(xs, p["w_gu"], b_gu, p["w_down"], b_down, tabs, l, dims)
        x = _combine(x1, colinfo, yb, lw["ln2_g"], lw["ln2_b"], tabs, dims, dn_alpha, split=l == depth - 1)
        outs["conv_p"].append(st_p[:, lo:, :])
        outs["h_p"].append(st_p[:, 0, :])
        outs["conv_s"].append(st_s[:, lo:, :])
        outs["h_s"].append(st_s[:, 0, :])
    st = {k: jnp.stack(v) for k, v in outs.items()}
    c_p, c_s, kr_p, kr_s = caches
    st["ckv_p"] = c_p.reshape(depth, dims.batch, dims.seq, KV_RANK)
    st["kr_p"] = kr_p.reshape(depth, dims.batch, dims.seq, ROPE_DIM)
    st["ckv_s"] = c_s.reshape(depth, dims.dec_batch, dims.dec_seq, KV_RANK)
    st["kr_s"] = kr_s.reshape(depth, dims.dec_batch, dims.dec_seq, ROPE_DIM)
    y_prompt = x[0].reshape(x_prompt.shape)
    y_sample = x[1].reshape(x_sample.shape)
    return (y_prompt, y_sample, st["ckv_p"], st["kr_p"], st["conv_p"], st["h_p"],
            st["ckv_s"], st["kr_s"], st["conv_s"], st["h_s"])


def kernel(x_prompt, x_sample, cache_ckv, cache_krope, state_conv, state_lru, w_in, conv_w, conv_b, w_rg, b_rg,
           w_ig, b_ig, lru_lambda, q_norm_g, w_uq, kv_norm_g, w_uk, w_uv, w_o, ln1_g, ln1_b, w_router, b_router,
           w_gu, b_gu, w_down, b_down, ln2_g, ln2_b):
    p = dict(w_in=w_in, conv_w=conv_w, conv_b=conv_b, w_rg=w_rg, b_rg=b_rg, w_ig=w_ig, b_ig=b_ig,
             lru_lambda=lru_lambda, q_norm_g=q_norm_g, w_uq=w_uq, kv_norm_g=kv_norm_g, w_uk=w_uk, w_uv=w_uv,
             w_o=w_o, ln1_g=ln1_g, ln1_b=ln1_b, w_router=w_router, b_router=b_router, w_gu=w_gu, b_gu=b_gu,
             w_down=w_down, b_down=b_down, ln2_g=ln2_g, ln2_b=ln2_b)
    return _trunk(x_prompt, x_sample, cache_ckv, cache_krope, state_conv, state_lru, p)
```

```python
import functools
from typing import NamedTuple

import jax
import jax.numpy as jnp
from jax import lax
from jax.experimental import pallas as pl
from jax.experimental.pallas import tpu as pltpu

F32 = jnp.float32
BF16 = jnp.bfloat16
I32 = jnp.int32

D_MODEL = 1024
CHUNK = 64
LRU_WIDTH = 512
CONV_W = 4
LRU_C = 8.0
MLA_HEADS = 8
NOPE_DIM = 64
ROPE_DIM = 32
V_DIM = 64
Q_RANK = 384
KV_RANK = 256
ROPE_THETA = 10000.0
N_EXPERTS = 32
TOP_K = 4
D_FF = 1024
SWIGLU_LIMIT = 7.0
SWIGLU_ALPHA = 1.702
LN_EPS = 1e-5
RMS_EPS = 1e-6
LOG2_E = 1.4426950408889634

V7X_SUBLANES = 8
V7X_LANES = 128
V7X_VMEM_BYTES = 64 * 1024 * 1024
MIB = 1024 * 1024

TM = 512
TQ = 512
TK = 1024
RT = 256
EBLK = 512
ESUB = 128
CH = V7X_SUBLANES
SLOTS = 1280
V_SLOT = 80
HEAD_SLOT = V7X_LANES

assert SLOTS >= RT * TOP_K + N_EXPERTS * (CH - 1) and SLOTS % CH == 0


class Dims(NamedTuple):
    depth: int
    batch: int
    seq: int
    dec_batch: int
    dec_seq: int
    past: int

    @property
    def n_prompt(self):
        return self.batch * self.seq

    @property
    def n_sample(self):
        return self.dec_batch * self.dec_seq

    @property
    def n_tok(self):
        return self.n_prompt + self.n_sample

    @property
    def n_rt(self):
        return self.n_tok // RT

    @property
    def n_eblk(self):
        chunks = self.n_tok * TOP_K // CH + self.n_rt * N_EXPERTS + N_EXPERTS * (EBLK // CH - 1)
        return -(-chunks // (EBLK // CH))


def _cparams(sem, nbytes):
    return pltpu.CompilerParams(dimension_semantics=sem,
                                vmem_limit_bytes=min(nbytes, V7X_VMEM_BYTES - 8 * MIB))


def _dot(a, b):
    return jnp.dot(a, b, preferred_element_type=F32)


def _dot_nt(a, b):
    return lax.dot_general(a, b, (((1,), (1,)), ((), ())), preferred_element_type=F32)


def _layer_norm(v, g, b):
    mu = jnp.mean(v, axis=-1, keepdims=True)
    vc = v - mu
    var = jnp.mean(vc * vc, axis=-1, keepdims=True)
    return vc * lax.rsqrt(var + LN_EPS) * g + b


def _rms_norm(v, g):
    return v * lax.rsqrt(jnp.mean(v * v, axis=-1, keepdims=True) + RMS_EPS) * g


def _inproj_kernel(xp_ref, xs_ref, win_ref, tqa_ref, tqb_ref, tabk_ref, qg_ref, kvg_ref, wqa_ref, wqb_ref, wkx_ref,
                   ekr_ref, wvt_ref, vone_ref, *rest, n_pt):
    ux_ref, ug_ref, cp_ref, cs_ref, krp_ref, krs_ref, qx_ref, kx_ref, vt_ref = rest[-9:]
    is_sample = pl.program_id(0) >= n_pt
    x = jnp.where(is_sample, xs_ref[...], xp_ref[...])
    u = _dot(x.astype(BF16), win_ref[...])
    o1, o2 = LRU_WIDTH, 2 * LRU_WIDTH
    o3 = o2 + Q_RANK
    o4 = o3 + KV_RANK
    ux_ref[...] = u[:, :o1]
    ug_ref[...] = u[:, o1:o2]
    c_new = _rms_norm(u[:, o3:o4], kvg_ref[...])
    cb = c_new.astype(BF16)
    tk = tabk_ref[...]
    kr = (u[:, o4:o4 + ROPE_DIM] * tk[:, :ROPE_DIM]
          + u[:, o4 + ROPE_DIM:o4 + 2 * ROPE_DIM] * tk[:, ROPE_DIM:])
    cs_ref[0] = c_new
    krs_ref[0] = kr
    kn = _dot(cb, wkx_ref[...])
    kr_slot = _dot(kr.astype(BF16), ekr_ref[...])
    vt_ref[...] = (_dot_nt(wvt_ref[...], cb) + vone_ref[...]).astype(BF16)
    qn = _rms_norm(u[:, o2:o3], qg_ref[...]).astype(BF16)
    qa = _dot(qn, wqa_ref[...])
    qb = _dot(qn, wqb_ref[...])
    ta = tqa_ref[...]
    tb = tqb_ref[...]
    for h in range(MLA_HEADS):
        hs = slice(h * HEAD_SLOT, (h + 1) * HEAD_SLOT)
        kx_ref[:, hs] = (kn[:, hs] + kr_slot).astype(BF16)
        qx_ref[:, hs] = (qa[:, hs] * ta + qb[:, hs] * tb).astype(BF16)

    @pl.when(jnp.logical_not(is_sample))
    def _():
        cp_ref[0] = c_new
        krp_ref[0] = kr


def _x_specs(x, n_pt):
    if isinstance(x, tuple):
        return list(x), [pl.BlockSpec((TM, D_MODEL), lambda i: (jnp.minimum(i, n_pt - 1), 0)),
                         pl.BlockSpec((TM, D_MODEL), lambda i: (0, 0))]
    return [x, x], [pl.BlockSpec((TM, D_MODEL), lambda i: (i, 0)), pl.BlockSpec((TM, D_MODEL), lambda i: (n_pt, 0))]


def _inproj(x, lw, tabs, caches, layer, dims):
    n = dims.n_tok
    n_pt = dims.n_prompt // TM
    x_args, x_specs = _x_specs(x, n_pt)
    tps = dims.seq // TM
    tab_map = lambda i: (jnp.where(i < n_pt, i % tps, tps), 0)
    row = lambda w: pl.BlockSpec((TM, w), lambda i: (i, 0))
    full = lambda a: pl.BlockSpec(a.shape, lambda i: (0,) * a.ndim)
    tab = lambda a: pl.BlockSpec((TM, a.shape[1]), tab_map)
    weights = [lw["qg"], lw["kvg"], lw["wqa"], lw["wqb"], lw["wkx"], lw["ekr"], lw["wvt"], lw["vone"]]
    hw = MLA_HEADS * HEAD_SLOT
    args = x_args + [lw["win"], tabs["tqa"], tabs["tqb"], tabs["tabk"]] + weights
    in_specs = (x_specs + [full(lw["win"]), tab(tabs["tqa"]), tab(tabs["tqb"]), tab(tabs["tabk"])]
                + [full(w) for w in weights])
    aliases = {}
    if caches is not None:
        aliases = {len(args) + k: 2 + k for k in range(4)}
        args = args + list(caches)
        in_specs = in_specs + [pl.BlockSpec(memory_space=pl.ANY)] * 4
    p_map = lambda i: (layer, jnp.minimum(i, n_pt - 1), 0)
    s_map = lambda i: (layer, 0, 0)
    return pl.pallas_call(
        functools.partial(_inproj_kernel, n_pt=n_pt),
        grid=(n // TM,),
        in_specs=in_specs,
        out_specs=[row(LRU_WIDTH), row(LRU_WIDTH),
                   pl.BlockSpec((1, TM, KV_RANK), p_map), pl.BlockSpec((1, TM, KV_RANK), s_map),
                   pl.BlockSpec((1, TM, ROPE_DIM), p_map), pl.BlockSpec((1, TM, ROPE_DIM), s_map),
                   row(hw), row(hw), pl.BlockSpec((MLA_HEADS * V_SLOT, TM), lambda i: (0, i))],
        out_shape=[jax.ShapeDtypeStruct((n, LRU_WIDTH), F32), jax.ShapeDtypeStruct((n, LRU_WIDTH), F32),
                   jax.ShapeDtypeStruct((dims.depth, dims.n_prompt, KV_RANK), F32),
                   jax.ShapeDtypeStruct((dims.depth, dims.n_sample, KV_RANK), F32),
                   jax.ShapeDtypeStruct((dims.depth, dims.n_prompt, ROPE_DIM), F32),
                   jax.ShapeDtypeStruct((dims.depth, dims.n_sample, ROPE_DIM), F32),
                   jax.ShapeDtypeStruct((n, hw), BF16), jax.ShapeDtypeStruct((n, hw), BF16),
                   jax.ShapeDtypeStruct((MLA_HEADS * V_SLOT, n), BF16)],
        input_output_aliases=aliases,
        compiler_params=_cparams(("arbitrary",), 48 * MIB),
        name="inproj",
    )(*args)


def _linear_scan(a, b, h0):
    n = a.shape[0]
    sub = lax.broadcasted_iota(I32, a.shape, 0) % V7X_SUBLANES
    d = 1
    while d < V7X_SUBLANES:
        keep = sub >= d
        a_sh = jnp.where(keep, pltpu.roll(a, d, 0), 1.0)
        b_sh = jnp.where(keep, pltpu.roll(b, d, 0), 0.0)
        b = b + a * b_sh
        a = a * a_sh
        d *= 2
    carry = h0
    groups = []
    for g in range(n // V7X_SUBLANES):
        rows = slice(g * V7X_SUBLANES, (g + 1) * V7X_SUBLANES)
        hg = b[rows] + a[rows] * carry
        groups.append(hg)
        carry = hg[V7X_SUBLANES - 1:V7X_SUBLANES]
    return jnp.concatenate(groups, axis=0)


def _sigmoid(v):
    return 0.5 + 0.5 * jnp.tanh(0.5 * v)


def _gelu_tanh(v):
    return 0.5 * v * (1.0 + jnp.tanh(0.7978845608028654 * (v + 0.044715 * v * v * v)))


def _lru_kernel(ux_ref, ug_ref, st0_ref, cw_ref, cb_ref, wrg_ref, brg_ref, wig_ref, big_ref,
                lam_ref, out_ref, st_ref, xs_sc, carry_sc, *, seg, tps):
    t = pl.program_id(0) % tps
    lo = V7X_SUBLANES - (CONV_W - 1)

    @pl.when(t == 0)
    def _():
        carry_sc[...] = st0_ref[0]

    st_ref[0] = jnp.zeros(st_ref.shape[1:], F32)

    xs_sc[lo:V7X_SUBLANES, :] = carry_sc[lo:V7X_SUBLANES, :]
    xs_sc[V7X_SUBLANES:V7X_SUBLANES + seg, :] = ux_ref[...]
    cw = cw_ref[...]
    xc = cb_ref[...] + xs_sc[lo:lo + seg, :] * cw[0:1, :]
    for k in range(1, CONV_W):
        xc = xc + xs_sc[lo + k:lo + k + seg, :] * cw[k:k + 1, :]
    new_conv = xs_sc[seg + lo:seg + V7X_SUBLANES, :]
    carry_sc[lo:V7X_SUBLANES, :] = new_conv
    st_ref[0, lo:V7X_SUBLANES, :] = new_conv

    xcb = xc.astype(BF16)
    half = LRU_WIDTH // 2
    for g in range(2):
        cs = slice(g * half, (g + 1) * half)
        xg = xc[:, cs]
        r = _sigmoid(_dot(xcb[:, cs], wrg_ref[g]) + brg_ref[:, cs])
        ig = _sigmoid(_dot(xcb[:, cs], wig_ref[g]) + big_ref[:, cs])
        nl = -lam_ref[:, cs]
        softplus = jnp.maximum(nl, 0.0) + jnp.log1p(jnp.exp(-jnp.abs(nl)))
        log_a = -LRU_C * r * softplus
        a = jnp.exp(log_a)
        gap = 1.0 - a * a
        bt = jnp.where(gap > 0.0, gap * lax.rsqrt(gap), 0.0) * ig * xg
        h = _linear_scan(a, bt, carry_sc[0:1, cs])
        h_last = h[seg - 1:seg, :]
        carry_sc[0:1, cs] = h_last
        st_ref[0, 0:1, cs] = h_last
        out_ref[:, cs] = (_gelu_tanh(ug_ref[:, cs]) * h).astype(BF16)


def _lru_state(conv, h):
    lo = V7X_SUBLANES - (CONV_W - 1)
    return jnp.concatenate([h[:, None, :], jnp.zeros((h.shape[0], lo - 1, h.shape[1]), F32), conv], axis=1)


def _lru(ux, ug, st0, lw, prev_out, *, seg, n_seg, tps, row0):
    n = ux.shape[0]
    blk0 = row0 // seg
    row = pl.BlockSpec((seg, LRU_WIDTH), lambda i: (blk0 + i, 0))
    full = lambda a: pl.BlockSpec(a.shape, lambda i: (0,) * a.ndim)
    n_seq = n_seg // tps
    weights = [lw["conv_w"], lw["conv_b"], lw["wrg"], lw["b_rg"], lw["wig"], lw["b_ig"], lw["lam"]]
    in_specs = [row, row,
                pl.BlockSpec((1, V7X_SUBLANES, LRU_WIDTH), lambda i: (i // tps, 0, 0))] + [full(w) for w in weights]
    args = [ux, ug, st0] + weights
    aliases = {}
    if prev_out is not None:
        in_specs.append(pl.BlockSpec(memory_space=pl.ANY))
        args.append(prev_out)
        aliases = {len(args) - 1: 0}

    def body(*refs):
        refs = list(refs)
        if prev_out is not None:
            del refs[len(args) - 1]
        _lru_kernel(*refs, seg=seg, tps=tps)

    return pl.pallas_call(
        body,
        grid=(n_seg,),
        in_specs=in_specs,
        out_specs=[row, pl.BlockSpec((1, V7X_SUBLANES, LRU_WIDTH), lambda i: (i // tps, 0, 0))],
        out_shape=[jax.ShapeDtypeStruct((n, LRU_WIDTH), BF16),
                   jax.ShapeDtypeStruct((n_seq, V7X_SUBLANES, LRU_WIDTH), F32)],
        scratch_shapes=[pltpu.VMEM((seg + V7X_SUBLANES, LRU_WIDTH), F32),
                        pltpu.VMEM((V7X_SUBLANES, LRU_WIDTH), F32)],
        input_output_aliases=aliases,
        compiler_params=_cparams(("arbitrary",), 40 * MIB),
        name="lru_seg%d" % seg,
    )(*args)


def _attn_prompt_kernel(qi_ref, kj_ref, mk_ref, qx_ref, kx_ref, vt_ref, o_ref, m_sc, acc_sc):
    pair = pl.program_id(1)
    i = qi_ref[pair]
    j = kj_ref[pair]
    flag = mk_ref[pair]

    @pl.when(j == 0)
    def _():
        m_sc[...] = jnp.full(m_sc.shape, -jnp.inf, F32)
        acc_sc[...] = jnp.zeros(acc_sc.shape, F32)

    def step(masked, n_keys):
        if masked:
            kpos = lax.broadcasted_iota(I32, (n_keys, TQ), 0) + j * TK
            qpos = lax.broadcasted_iota(I32, (n_keys, TQ), 1) + i * TQ
            visible = kpos < (qpos // CHUNK + 1) * CHUNK

        def scores(h):
            hs = slice(h * HEAD_SLOT, (h + 1) * HEAD_SLOT)
            return _dot_nt(kx_ref[:n_keys, hs], qx_ref[:, hs])

        pending = [scores(0), scores(1)]
        for h in range(MLA_HEADS):
            s = pending.pop(0)
            if h + 2 < MLA_HEADS:
                pending.append(scores(h + 2))
            if masked:
                s = jnp.where(visible, s, -jnp.inf)
            m_prev = m_sc[h]
            m_new = jnp.maximum(m_prev, jnp.max(s, axis=0, keepdims=True))
            alpha = jnp.exp2(m_prev - m_new)
            p = jnp.exp2(s - m_new).astype(BF16)
            acc_sc[h] = alpha * acc_sc[h] + _dot(vt_ref[h * V_SLOT:(h + 1) * V_SLOT, :n_keys], p)
            m_sc[h] = m_new

    @pl.when((flag & 5) == 0)
    def _():
        step(False, TK)

    @pl.when((flag & 5) == 1)
    def _():
        step(True, TK)

    if TK > TQ:
        @pl.when((flag & 5) == 5)
        def _():
            step(True, TQ)

    @pl.when((flag & 2) == 2)
    def _():
        o_t = jnp.concatenate([acc_sc[h, :V_DIM, :] / acc_sc[h, V_DIM:V_DIM + 1, :] for h in range(MLA_HEADS)],
                              axis=0)
        o_ref[...] = o_t.T.astype(BF16)


def _attn_prompt(qx, kx, vt, dims):
    n = dims.n_tok
    nq = dims.seq // TQ
    nk = dims.seq // TK
    pairs = []
    for i in range(nq):
        n_kv = -(-(i + 1) * TQ // TK)
        for j in range(n_kv):
            short = TK > TQ and (i + 1) * TQ <= j * TK + TQ
            pairs.append((i, j, int((j + 1) * TK > i * TQ) | (2 if j == n_kv - 1 else 0) | (4 if short else 0)))
    qi = jnp.asarray([p[0] for p in pairs], I32)
    kj = jnp.asarray([p[1] for p in pairs], I32)
    mk = jnp.asarray([p[2] for p in pairs], I32)
    hw = MLA_HEADS * HEAD_SLOT
    return pl.pallas_call(
        _attn_prompt_kernel,
        grid_spec=pltpu.PrefetchScalarGridSpec(
            num_scalar_prefetch=3,
            grid=(dims.batch, len(pairs)),
            in_specs=[pl.BlockSpec((TQ, hw), lambda b, p, qi, kj, mk: (b * nq + qi[p], 0)),
                      pl.BlockSpec((TK, hw), lambda b, p, qi, kj, mk: (b * nk + kj[p], 0)),
                      pl.BlockSpec((MLA_HEADS * V_SLOT, TK), lambda b, p, qi, kj, mk: (0, b * nk + kj[p]))],
            out_specs=pl.BlockSpec((TQ, MLA_HEADS * V_DIM), lambda b, p, qi, kj, mk: (b * nq + qi[p], 0)),
            scratch_shapes=[pltpu.VMEM((MLA_HEADS, 1, TQ), F32), pltpu.VMEM((MLA_HEADS, V_SLOT, TQ), F32)],
        ),
        out_shape=jax.ShapeDtypeStruct((n, MLA_HEADS * V_DIM), BF16),
        compiler_params=_cparams(("parallel", "arbitrary"), 48 * MIB),
        name="attn_prompt",
    )(qi, kj, mk, qx, kx, vt)


def _attn_sample_kernel(qx_ref, pk_ref, pkr_ref, c_ref, kr_ref, wabs_ref, ekr_ref, wuv_ref, prev_ref, o_ref,
                        *, dec_seq):
    del prev_ref
    slots = [qx_ref[:, h * HEAD_SLOT:(h + 1) * HEAD_SLOT] for h in range(MLA_HEADS)]
    q_slot = jnp.concatenate(slots, axis=0)
    q_lat = jnp.concatenate([_dot(slots[h], wabs_ref[h]) for h in range(MLA_HEADS)], axis=0).astype(BF16)
    ekr = ekr_ref[...]
    pk = pk_ref[0, 0].astype(BF16)
    pkr = _dot(pkr_ref[0, 0].astype(BF16), ekr).astype(BF16)
    k = c_ref[0].astype(BF16)
    kr = _dot(kr_ref[0].astype(BF16), ekr).astype(BF16)
    s_past = _dot_nt(q_lat, pk) + _dot_nt(q_slot, pkr)
    s_new = _dot_nt(q_lat, k) + _dot_nt(q_slot, kr)
    m = jnp.maximum(jnp.max(s_past, axis=1, keepdims=True), jnp.max(s_new, axis=1, keepdims=True))
    p_past = jnp.exp2(s_past - m)
    p_new = jnp.exp2(s_new - m)
    denom = jnp.sum(p_past, axis=1, keepdims=True) + jnp.sum(p_new, axis=1, keepdims=True)
    o_lat = (_dot(p_past.astype(BF16), pk) + _dot(p_new.astype(BF16), k)) / denom
    for h in range(MLA_HEADS):
        o_h = o_lat[h * dec_seq:(h + 1) * dec_seq, :].astype(BF16)
        o_ref[:, h * V_DIM:(h + 1) * V_DIM] = _dot(o_h, wuv_ref[h]).astype(BF16)


def _attn_sample(qx, cache_ckv, cache_krope, c_s, kr_s, lw, attn_prev, layer, dims):
    ds = dims.dec_seq
    blk0 = dims.n_prompt // ds
    rmap = lambda s: (blk0 + s, 0)
    smap = lambda s: (layer, s, 0)
    past = cache_ckv.shape[2]
    full = lambda a: pl.BlockSpec(a.shape, lambda s: (0,) * a.ndim)
    return pl.pallas_call(
        functools.partial(_attn_sample_kernel, dec_seq=ds),
        grid=(dims.dec_batch,),
        in_specs=[pl.BlockSpec((ds, MLA_HEADS * HEAD_SLOT), rmap),
                  pl.BlockSpec((1, 1, past, KV_RANK), lambda s: (layer, s, 0, 0)),
                  pl.BlockSpec((1, 1, past, ROPE_DIM), lambda s: (layer, s, 0, 0)),
                  pl.BlockSpec((1, ds, KV_RANK), smap), pl.BlockSpec((1, ds, ROPE_DIM), smap),
                  full(lw["wabs"]), full(lw["ekr"]), full(lw["wuv"]),
                  pl.BlockSpec(memory_space=pl.ANY)],
        out_specs=pl.BlockSpec((ds, MLA_HEADS * V_DIM), rmap),
        out_shape=jax.ShapeDtypeStruct(attn_prev.shape, attn_prev.dtype),
        input_output_aliases={8: 0},
        compiler_params=_cparams(("parallel",), 48 * MIB),
        name="attn_sample",
    )(qx, cache_ckv, cache_krope, c_s, kr_s, lw["wabs"], lw["ekr"], lw["wuv"], attn_prev)


def _outproj_kernel(xp_ref, xs_ref, lru_ref, attn_ref, wo_ref, g_ref, b_ref, wr_ref, br_ref, tri_ref, low_ref,
                    o_ref, row_ref, col_ref, cnt_ref, *, dn_alpha, n_pt):
    x = jnp.where(pl.program_id(0) >= n_pt, xs_ref[...], xp_ref[...])
    mix = _dot(lru_ref[...], wo_ref[:LRU_WIDTH, :]) + _dot(attn_ref[...], wo_ref[LRU_WIDTH:, :])
    x1 = _layer_norm(dn_alpha * x + mix, g_ref[...], b_ref[...])
    o_ref[...] = x1
    for r in range(TM // RT):
        info, cnt = _route_tile(x1[r * RT:(r + 1) * RT, :], wr_ref[...], br_ref[...], tri_ref[...], low_ref[...])
        row_ref[r] = info
        pad = jnp.zeros((V7X_LANES - 2 * TOP_K, RT), F32)
        col_ref[r] = jnp.concatenate([info, pad], axis=0).T
        cnt_ref[r] = jnp.broadcast_to(cnt, cnt_ref.shape[1:])


def _outproj(x, lru_out, attn, lw, tri, low, dims, dn_alpha):
    n = dims.n_tok
    n_pt = dims.n_prompt // TM
    per = TM // RT
    x_args, x_specs = _x_specs(x, n_pt)
    row = lambda w: pl.BlockSpec((TM, w), lambda i: (i, 0))
    full = lambda a: pl.BlockSpec(a.shape, lambda i: (0,) * a.ndim)
    consts = [lw["wo"], lw["ln1_g"], lw["ln1_b"], lw["wr_t"], lw["br"], tri, low]
    return pl.pallas_call(
        functools.partial(_outproj_kernel, dn_alpha=dn_alpha, n_pt=n_pt),
        grid=(n // TM,),
        in_specs=x_specs + [row(LRU_WIDTH), row(MLA_HEADS * V_DIM)] + [full(c) for c in consts],
        out_specs=[row(D_MODEL),
                   pl.BlockSpec((per, 2 * TOP_K, RT), lambda i: (i, 0, 0)),
                   pl.BlockSpec((per, RT, V7X_LANES), lambda i: (i, 0, 0)),
                   pl.BlockSpec((per, N_EXPERTS, V7X_LANES), lambda i: (i, 0, 0))],
        out_shape=[jax.ShapeDtypeStruct((n, D_MODEL), F32),
                   jax.ShapeDtypeStruct((dims.n_rt, 2 * TOP_K, RT), F32),
                   jax.ShapeDtypeStruct((dims.n_rt, RT, V7X_LANES), F32),
                   jax.ShapeDtypeStruct((dims.n_rt, N_EXPERTS, V7X_LANES), F32)],
        compiler_params=_cparams(("parallel",), 40 * MIB),
        name="outproj",
    )(*x_args, lru_out, attn, *consts)


def _split_bf16(v):
    hi = v.astype(BF16)
    return hi, (v - hi.astype(F32)).astype(BF16)


def _route_tile(x, wr, br, tri, low):
    x_hi, x_lo = _split_bf16(x)
    w_hi, w_lo = _split_bf16(wr)
    logits = _dot_nt(w_hi, x_hi) + (_dot_nt(w_hi, x_lo) + _dot_nt(w_lo, x_hi)) + br
    eidx = lax.broadcasted_iota(I32, logits.shape, 0)
    work = logits
    onehots, vals = [], []
    for _ in range(TOP_K):
        m = jnp.max(work, axis=0, keepdims=True)
        first = jnp.min(jnp.where(work == m, eidx, N_EXPERTS), axis=0, keepdims=True)
        oh = eidx == first
        onehots.append(oh)
        vals.append(m)
        work = jnp.where(oh, -jnp.inf, work)
    ex = [jnp.exp(v - vals[0]) for v in vals]
    denom = ex[0] + ex[1] + ex[2] + ex[3]
    sel = onehots[0] | onehots[1] | onehots[2] | onehots[3]
    self32 = sel.astype(F32)
    rank = _dot(self32.astype(BF16), tri)
    cnt = jnp.sum(self32, axis=1, keepdims=True)
    padded = jnp.floor((cnt + (CH - 1)) * (1.0 / CH)) * CH
    lstart = _dot(low, jnp.broadcast_to(padded, (N_EXPERTS, V7X_LANES)).astype(BF16))[:, 0:1]
    slotmat = lstart + rank
    rows = [jnp.sum(jnp.where(oh, slotmat, 0.0), axis=0, keepdims=True) for oh in onehots]
    rows += [e / denom for e in ex]
    return jnp.concatenate(rows, axis=0), cnt


def _routing_tables(cnt, dims):
    cpb = EBLK // CH
    nch = (cnt.astype(I32) + (CH - 1)) // CH
    tot = jnp.sum(nch, axis=0)
    nblk = (tot + cpb - 1) // cpb
    blk_end = jnp.cumsum(nblk)
    base = (blk_end - nblk) * cpb
    gstart = base[None, :] + jnp.cumsum(nch, axis=0) - nch
    blk_ids = jnp.arange(dims.n_eblk, dtype=I32)
    blk_e = jnp.minimum(jnp.sum((blk_end[None, :] <= blk_ids[:, None]).astype(I32), axis=1), N_EXPERTS - 1)
    e_ids = jnp.arange(N_EXPERTS, dtype=I32)
    owner = (blk_e[:, None] == e_ids[None, :]).astype(I32)
    first_blk = jnp.sum(owner * (blk_end - nblk)[None, :], axis=1)
    blk_valid = jnp.clip(jnp.sum(owner * tot[None, :], axis=1) * CH - (blk_ids - first_blk) * EBLK,
                         0, EBLK).astype(I32)
    later_used = (e_ids[None, :] > e_ids[:, None]) & (nblk[None, :] > 0)
    next_e = jnp.min(jnp.where(later_used, e_ids[None, :], N_EXPERTS), axis=1)
    next_e = jnp.where(next_e < N_EXPERTS, next_e, -1).astype(I32)
    return dict(nch=nch.reshape(-1), gstart=gstart.astype(I32).reshape(-1), ntot=jnp.sum(nch, axis=1).astype(I32),
                next_e=next_e, blk_valid=blk_valid,
                tail_start=(base + tot).astype(I32), tail_n=(nblk * cpb - tot).astype(I32),
                blk_e=blk_e, nb_used=blk_end[-1:].astype(I32))


def _group_chunks(nch_ref, gst_ref, tile, visit):
    l0 = jnp.int32(0)
    for e in range(N_EXPERTS):
        n = nch_ref[tile * N_EXPERTS + e]
        g0 = gst_ref[tile * N_EXPERTS + e]

        def body(c, carry, l0=l0, g0=g0):
            visit(l0 + c, g0 + c)
            return carry

        lax.fori_loop(0, n, body, 0)
        l0 = l0 + n
    return l0


def _chunk(ref, c):
    if isinstance(c, int):
        return ref.at[pl.ds(c * CH, CH)]
    return ref.at[pl.ds(pl.multiple_of(c * CH, CH), CH)]


def _wait_chunks(vmem_ref, hbm_ref, sem, count):
    @pl.when(count > 0)
    def _():
        rows = pl.ds(0, count * CH)
        pltpu.make_async_copy(vmem_ref.at[rows], hbm_ref.at[rows], sem).wait()


def _dispatch_kernel(nch_ref, gst_ref, ntot_ref, tls_ref, tln_ref, x_ref, row_ref, xs_ref, loc_sc, zero_sc, sems):
    i = pl.program_id(0)
    last = pl.num_programs(0) - 1
    slot = i % 2

    @pl.when(i >= 2)
    def _():
        _wait_chunks(loc_sc.at[slot], xs_ref, sems.at[slot], ntot_ref[jnp.maximum(i - 2, 0)])

    slots = row_ref[0]
    srow = lax.broadcasted_iota(I32, (SLOTS, RT), 0).astype(F32)
    perm = jnp.zeros((SLOTS, RT), F32)
    for k in range(TOP_K):
        perm = jnp.where(srow == slots[k:k + 1], 1.0, perm)
    loc_sc[slot] = _dot(perm.astype(BF16), x_ref[...].astype(BF16))
    loc = loc_sc.at[slot]
    _group_chunks(nch_ref, gst_ref, i,
                  lambda lc, gc: pltpu.make_async_copy(_chunk(loc, lc), _chunk(xs_ref, gc), sems.at[slot]).start())

    @pl.when(i == last)
    def _():
        _wait_chunks(loc, xs_ref, sems.at[slot], ntot_ref[i])

        @pl.when(i >= 1)
        def _():
            _wait_chunks(loc_sc.at[1 - slot], xs_ref, sems.at[1 - slot], ntot_ref[jnp.maximum(i - 1, 0)])

        zero_sc[...] = jnp.zeros(zero_sc.shape, F32)
        n_tail = jnp.int32(0)
        for e in range(N_EXPERTS):
            n = tln_ref[e]
            g0 = tls_ref[e]

            def body(c, carry, g0=g0):
                pltpu.make_async_copy(zero_sc, _chunk(xs_ref, g0 + c), sems.at[0]).start()
                return carry

            lax.fori_loop(0, n, body, 0)
            n_tail = n_tail + n

        def wait_tail(c, carry):
            pltpu.make_async_copy(zero_sc, _chunk(xs_ref, 0), sems.at[0]).wait()
            return carry

        lax.fori_loop(0, n_tail, wait_tail, 0)


def _dispatch(x1, rowinfo, tabs, dims):
    rows = dims.n_eblk * EBLK
    return pl.pallas_call(
        _dispatch_kernel,
        grid_spec=pltpu.PrefetchScalarGridSpec(
            num_scalar_prefetch=5,
            grid=(dims.n_rt,),
            in_specs=[pl.BlockSpec((RT, D_MODEL), lambda i, *_: (i, 0)),
                      pl.BlockSpec((1, 2 * TOP_K, RT), lambda i, *_: (i, 0, 0))],
            out_specs=pl.BlockSpec(memory_space=pl.ANY),
            scratch_shapes=[pltpu.VMEM((2, SLOTS, D_MODEL), F32), pltpu.VMEM((CH, D_MODEL), F32),
                            pltpu.SemaphoreType.DMA((2,))],
        ),
        out_shape=jax.ShapeDtypeStruct((rows, D_MODEL), F32),
        compiler_params=_cparams(("arbitrary",), 40 * MIB),
        name="dispatch",
    )(tabs["nch"], tabs["gstart"], tabs["ntot"], tabs["tail_start"], tabs["tail_n"], x1, rowinfo)


def _expert_kernel(be_ref, nb_ref, nxt_ref, bv_ref, xs_ref, wgu_hbm, bgu_ref, wd_hbm, bd_ref, y_ref,
                   wgu_st, wd_st, wgu_sc, wd_sc, sems, *, layer):
    b = pl.program_id(0)

    def copies(e):
        return (pltpu.make_async_copy(wgu_hbm.at[layer, e], wgu_st, sems.at[0]),
                pltpu.make_async_copy(wd_hbm.at[layer, e], wd_st, sems.at[1]))

    def mlp(n_rows):
        rows = slice(0, n_rows)
        h = _dot(xs_ref[rows, :].astype(BF16), wgu_sc[...]) + bgu_ref[0, 0]
        hg = jnp.minimum(h[:, :D_FF], SWIGLU_LIMIT)
        hl = jnp.clip(h[:, D_FF:], -SWIGLU_LIMIT, SWIGLU_LIMIT)
        act = hg * jax.nn.sigmoid(SWIGLU_ALPHA * hg) * (hl + 1.0)
        y_ref[rows, :] = _dot(act.astype(BF16), wd_sc[...]) + bd_ref[0, 0]

    @pl.when(b < nb_ref[0])
    def _():
        e = be_ref[b]

        @pl.when(b == 0)
        def _():
            for c in copies(e):
                c.start()

        @pl.when((b == 0) | (e != be_ref[jnp.maximum(b - 1, 0)]))
        def _():
            for c in copies(e):
                c.wait()
            wgu_sc[...] = wgu_st[...].astype(BF16)
            wd_sc[...] = wd_st[...].astype(BF16)
            nxt = nxt_ref[e]

            @pl.when(nxt >= 0)
            def _():
                for c in copies(nxt):
                    c.start()

        valid = bv_ref[b]
        for n_rows in range(ESUB, EBLK + 1, ESUB):
            @pl.when((valid > n_rows - ESUB) & (valid <= n_rows))
            def _(n_rows=n_rows):
                mlp(n_rows)


def _experts(xs, w_gu, b_gu, w_down, b_down, tabs, layer, dims):
    blk = lambda b, be, nb, *_: jnp.minimum(b, nb[0] - 1)
    wmap = lambda b, be, nb, *_: (layer, be[blk(b, be, nb)], 0, 0)
    row = pl.BlockSpec((EBLK, D_MODEL), lambda b, be, nb, *_: (blk(b, be, nb), 0))
    return pl.pallas_call(
        functools.partial(_expert_kernel, layer=layer),
        grid_spec=pltpu.PrefetchScalarGridSpec(
            num_scalar_prefetch=4,
            grid=(dims.n_eblk,),
            in_specs=[row, pl.BlockSpec(memory_space=pl.ANY), pl.BlockSpec((1, 1, 1, 2 * D_FF), wmap),
                      pl.BlockSpec(memory_space=pl.ANY), pl.BlockSpec((1, 1, 1, D_MODEL), wmap)],
            out_specs=row,
            scratch_shapes=[pltpu.VMEM((D_MODEL, 2 * D_FF), F32), pltpu.VMEM((D_FF, D_MODEL), F32),
                            pltpu.VMEM((D_MODEL, 2 * D_FF), BF16), pltpu.VMEM((D_FF, D_MODEL), BF16),
                            pltpu.SemaphoreType.DMA((2,))],
        ),
        out_shape=jax.ShapeDtypeStruct(xs.shape, F32),
        compiler_params=_cparams(("arbitrary",), 54 * MIB),
        name="experts",
    )(tabs["blk_e"], tabs["nb_used"], tabs["next_e"], tabs["blk_valid"], xs, w_gu, b_gu, w_down, b_down)


def _combine_kernel(nch_ref, gst_ref, ntot_ref, x_ref, col_ref, yb_ref, g_ref, b_ref, *rest, dn_alpha, n_prt):
    o_refs, (loc_sc, sems) = rest[:-2], rest[-2:]
    i = pl.program_id(0)
    slot = i % 2

    def fetch(tile, sl):
        loc = loc_sc.at[sl]
        total = _group_chunks(
            nch_ref, gst_ref, tile,
            lambda lc, gc: pltpu.make_async_copy(_chunk(yb_ref, gc), _chunk(loc, lc), sems.at[sl]).start())

        def zero(c, carry):
            _chunk(loc, c)[...] = jnp.zeros((CH, D_MODEL), F32)
            return carry

        lax.fori_loop(total, SLOTS // CH, zero, 0)

    @pl.when(i == 0)
    def _():
        fetch(i, slot)

    @pl.when(i + 1 < pl.num_programs(0))
    def _():
        fetch(i + 1, 1 - slot)

    info = col_ref[0]
    scol = lax.broadcasted_iota(I32, (RT, SLOTS), 1).astype(F32)
    gmat = jnp.zeros((RT, SLOTS), F32)
    for k in reversed(range(TOP_K)):
        gmat = jnp.where(scol == info[:, k:k + 1], info[:, TOP_K + k:TOP_K + k + 1], gmat)

    _wait_chunks(loc_sc.at[slot], yb_ref, sems.at[slot], ntot_ref[i])
    y = _dot(gmat.astype(BF16), loc_sc[slot].astype(BF16))
    out = _layer_norm(dn_alpha * x_ref[...] + y, g_ref[...], b_ref[...])
    if n_prt is None:
        o_refs[0][...] = out
    else:
        @pl.when(i < n_prt)
        def _():
            o_refs[0][...] = out

        @pl.when(i >= n_prt)
        def _():
            o_refs[1][...] = out


def _combine(x1, colinfo, yb, g, b, tabs, dims, dn_alpha, split):
    n = dims.n_tok
    n_prt = dims.n_prompt // RT
    if split:
        out_specs = [pl.BlockSpec((RT, D_MODEL), lambda i, *_: (jnp.minimum(i, n_prt - 1), 0)),
                     pl.BlockSpec((RT, D_MODEL), lambda i, *_: (jnp.maximum(i - n_prt, 0), 0))]
        out_shape = [jax.ShapeDtypeStruct((dims.n_prompt, D_MODEL), F32),
                     jax.ShapeDtypeStruct((dims.n_sample, D_MODEL), F32)]
    else:
        out_specs = pl.BlockSpec((RT, D_MODEL), lambda i, *_: (i, 0))
        out_shape = jax.ShapeDtypeStruct((n, D_MODEL), F32)
    return pl.pallas_call(
        functools.partial(_combine_kernel, dn_alpha=dn_alpha, n_prt=n_prt if split else None),
        grid_spec=pltpu.PrefetchScalarGridSpec(
            num_scalar_prefetch=3,
            grid=(dims.n_rt,),
            in_specs=[pl.BlockSpec((RT, D_MODEL), lambda i, *_: (i, 0)),
                      pl.BlockSpec((1, RT, V7X_LANES), lambda i, *_: (i, 0, 0)),
                      pl.BlockSpec(memory_space=pl.ANY),
                      pl.BlockSpec(g.shape, lambda i, *_: (0, 0)), pl.BlockSpec(b.shape, lambda i, *_: (0, 0))],
            out_specs=out_specs,
            scratch_shapes=[pltpu.VMEM((2, SLOTS, D_MODEL), F32), pltpu.SemaphoreType.DMA((2,))],
        ),
        out_shape=out_shape,
        compiler_params=_cparams(("arbitrary",), 40 * MIB),
        name="combine",
    )(tabs["nch"], tabs["gstart"], tabs["ntot"], x1, colinfo, yb, g, b)


def _swap_halves(w, group):
    shp = w.shape
    w = w.reshape(shp[:-1] + (shp[-1] // group, 2, group // 2))
    return w[..., ::-1, :].reshape(shp)


def _block_diag(blocks, per):
    h, a, b = blocks.shape
    grouped = blocks.reshape(h // per, per, a, b)
    out = jnp.zeros((h // per, per * a, per * b), blocks.dtype)
    for p in range(per):
        out = out.at[:, p * a:(p + 1) * a, p * b:(p + 1) * b].set(grouped[:, p])
    return out


def _rope_tables(dims):
    half = ROPE_DIM // 2
    pos = jnp.concatenate([jnp.arange(dims.seq, dtype=I32),
                           jnp.tile(dims.past + jnp.arange(dims.dec_seq, dtype=I32), dims.dec_batch)]).astype(F32)
    freqs = ROPE_THETA ** (-2.0 * jnp.arange(half, dtype=F32) / ROPE_DIM)
    ang = pos[:, None] * freqs[None, :]
    cos, sin = jnp.cos(ang), jnp.sin(ang)
    cos2 = jnp.concatenate([cos, cos], axis=1)
    sin2 = jnp.concatenate([-sin, sin], axis=1)
    scale = (NOPE_DIM + ROPE_DIM) ** -0.5 * LOG2_E
    n = pos.shape[0]
    pad = jnp.zeros((n, HEAD_SLOT - NOPE_DIM - ROPE_DIM), F32)
    tqa = jnp.concatenate([jnp.full((n, NOPE_DIM), scale, F32), cos2 * scale, pad], axis=1)
    tqb = jnp.concatenate([jnp.zeros((n, NOPE_DIM), F32), sin2 * scale, pad], axis=1)
    return dict(tqa=tqa, tqb=tqb, tabk=jnp.concatenate([cos2, sin2], axis=1))


def _head_slots(nope, rope):
    r = nope.shape[0]
    pad = jnp.zeros((r, MLA_HEADS, HEAD_SLOT - NOPE_DIM - ROPE_DIM), nope.dtype)
    return jnp.concatenate([nope, rope, pad], axis=2).reshape(r, MLA_HEADS * HEAD_SLOT)


def _layer_weights(p, l):
    o4 = 2 * LRU_WIDTH + Q_RANK + KV_RANK
    w_in = p["w_in"][l]
    win = jnp.concatenate([w_in, _swap_halves(w_in[:, o4:], ROPE_DIM)], axis=1).astype(BF16)
    w_uq = p["w_uq"][l].reshape(Q_RANK, MLA_HEADS, NOPE_DIM + ROPE_DIM)
    uq_nope, uq_rope = w_uq[:, :, :NOPE_DIM], w_uq[:, :, NOPE_DIM:]
    w_uk = p["w_uk"][l]
    zero_rope = jnp.zeros((KV_RANK, MLA_HEADS, ROPE_DIM), F32)
    ekr = jnp.zeros((ROPE_DIM, HEAD_SLOT), F32).at[jnp.arange(ROPE_DIM), NOPE_DIM + jnp.arange(ROPE_DIM)].set(1.0)
    wabs = jnp.pad(jnp.transpose(w_uk, (1, 2, 0)), ((0, 0), (0, HEAD_SLOT - NOPE_DIM), (0, 0)))
    return dict(
        win=win,
        wqa=_head_slots(uq_nope, uq_rope).astype(BF16),
        wqb=_head_slots(jnp.zeros_like(uq_nope), _swap_halves(uq_rope, ROPE_DIM)).astype(BF16),
        wkx=_head_slots(w_uk, zero_rope).astype(BF16),
        ekr=ekr.astype(BF16),
        wvt=jnp.pad(jnp.transpose(p["w_uv"][l], (1, 2, 0)), ((0, 0), (0, V_SLOT - V_DIM), (0, 0))
                    ).reshape(MLA_HEADS * V_SLOT, KV_RANK).astype(BF16),
        vone=jnp.tile((jnp.arange(V_SLOT) == V_DIM).astype(F32), MLA_HEADS)[:, None],
        wabs=wabs.astype(BF16),
        wuv=jnp.transpose(p["w_uv"][l], (1, 0, 2)).astype(BF16),
        qg=p["q_norm_g"][l][None, :], kvg=p["kv_norm_g"][l][None, :],
        conv_w=p["conv_w"][l], conv_b=p["conv_b"][l][None, :],
        wrg=_block_diag(p["w_rg"][l], 4).astype(BF16), b_rg=p["b_rg"][l][None, :],
        wig=_block_diag(p["w_ig"][l], 4).astype(BF16), b_ig=p["b_ig"][l][None, :],
        lam=p["lru_lambda"][l][None, :],
        wo=p["w_o"][l].astype(BF16), ln1_g=p["ln1_g"][l][None, :], ln1_b=p["ln1_b"][l][None, :],
        wr_t=p["w_router"][l].T, br=p["b_router"][l][:, None],
        ln2_g=p["ln2_g"][l][None, :], ln2_b=p["ln2_b"][l][None, :],
    )


def _trunk(x_prompt, x_sample, cache_ckv, cache_krope, state_conv, state_lru, p):
    depth = p["w_in"].shape[0]
    dims = Dims(depth, x_prompt.shape[0], x_prompt.shape[1], x_sample.shape[0], x_sample.shape[1],
                cache_ckv.shape[2])
    assert dims.n_sample == TM and dims.seq % TM == 0 and dims.dec_seq == CHUNK and dims.n_tok % RT == 0
    dn_alpha = (2.0 * depth) ** 0.25
    x = (x_prompt.reshape(-1, D_MODEL), x_sample.reshape(-1, D_MODEL))
    rope_tabs = _rope_tables(dims)
    tri = (jnp.arange(RT)[:, None] < jnp.arange(RT)[None, :]).astype(BF16)
    low = (jnp.arange(N_EXPERTS)[None, :] < jnp.arange(N_EXPERTS)[:, None]).astype(BF16)
    b_gu = p["b_gu"][:, :, None, :]
    b_down = p["b_down"][:, :, None, :]
    zero_state = jnp.zeros((dims.batch, V7X_SUBLANES, LRU_WIDTH), F32)
    tps = dims.seq // TM
    n_p = dims.n_prompt
    lo = V7X_SUBLANES - (CONV_W - 1)
    outs = {k: [] for k in ("conv_p", "h_p", "conv_s", "h_s")}
    caches = None
    for l in range(depth):
        lw = _layer_weights(p, l)
        ux, ug, *caches, qx, kx, vt = _inproj(x, lw, rope_tabs, caches, l, dims)
        lru_out, st_p = _lru(ux, ug, zero_state, lw, None,
                             seg=TM, n_seg=dims.batch * tps, tps=tps, row0=0)
        lru_out, st_s = _lru(ux, ug, _lru_state(state_conv[l], state_lru[l]), lw, lru_out,
                             seg=dims.dec_seq, n_seg=dims.dec_batch, tps=1, row0=n_p)
        attn = _attn_prompt(qx, kx, vt, dims)
        attn = _attn_sample(qx, cache_ckv, cache_krope, caches[1], caches[3], lw, attn, l, dims)
        x1, rowinfo, colinfo, cnt = _outproj(x, lru_out, attn, lw, tri, low, dims, dn_alpha)
        tabs = _routing_tables(cnt[:, :, 0], dims)
        xs = _dispatch(x1, rowinfo, tabs, dims)
        yb = _experts(xs, p["w_gu"], b_gu, p["w_down"], b_down, tabs, l, dims)
        x = _combine(x1, colinfo, yb, lw["ln2_g"], lw["ln2_b"], tabs, dims, dn_alpha, split=l == depth - 1)
        outs["conv_p"].append(st_p[:, lo:, :])
        outs["h_p"].append(st_p[:, 0, :])
        outs["conv_s"].append(st_s[:, lo:, :])
        outs["h_s"].append(st_s[:, 0, :])
    st = {k: jnp.stack(v) for k, v in outs.items()}
    c_p, c_s, kr_p, kr_s = caches
    st["ckv_p"] = c_p.reshape(depth, dims.batch, dims.seq, KV_RANK)
    st["kr_p"] = kr_p.reshape(depth, dims.batch, dims.seq, ROPE_DIM)
    st["ckv_s"] = c_s.reshape(depth, dims.dec_batch, dims.dec_seq, KV_RANK)
    st["kr_s"] = kr_s.reshape(depth, dims.dec_batch, dims.dec_seq, ROPE_DIM)
    y_prompt = x[0].reshape(x_prompt.shape)
    y_sample = x[1].reshape(x_sample.shape)
    return (y_prompt, y_sample, st["ckv_p"], st["kr_p"], st["conv_p"], st["h_p"],
            st["ckv_s"], st["kr_s"], st["conv_s"], st["h_s"])


def kernel(x_prompt, x_sample, cache_ckv, cache_krope, state_conv, state_lru, w_in, conv_w, conv_b, w_rg, b_rg,
           w_ig, b_ig, lru_lambda, q_norm_g, w_uq, kv_norm_g, w_uk, w_uv, w_o, ln1_g, ln1_b, w_router, b_router,
           w_gu, b_gu, w_down, b_down, ln2_g, ln2_b):
    p = dict(w_in=w_in, conv_w=conv_w, conv_b=conv_b, w_rg=w_rg, b_rg=b_rg, w_ig=w_ig, b_ig=b_ig,
             lru_lambda=lru_lambda, q_norm_g=q_norm_g, w_uq=w_uq, kv_norm_g=kv_norm_g, w_uk=w_uk, w_uv=w_uv,
             w_o=w_o, ln1_g=ln1_g, ln1_b=ln1_b, w_router=w_router, b_router=b_router, w_gu=w_gu, b_gu=b_gu,
             w_down=w_down, b_down=b_down, ln2_g=ln2_g, ln2_b=ln2_b)
    return _trunk(x_prompt, x_sample, cache_ckv, cache_krope, state_conv, state_lru, p)
```

```python
import functools
from typing import NamedTuple

import jax
import jax.numpy as jnp
from jax import lax
from jax.experimental import pallas as pl
from jax.experimental.pallas import tpu as pltpu

F32 = jnp.float32
BF16 = jnp.bfloat16
I32 = jnp.int32

D_MODEL = 1024
CHUNK = 64
LRU_WIDTH = 512
CONV_W = 4
LRU_C = 8.0
MLA_HEADS = 8
NOPE_DIM = 64
ROPE_DIM = 32
V_DIM = 64
Q_RANK = 384
KV_RANK = 256
ROPE_THETA = 10000.0
N_EXPERTS = 32
TOP_K = 4
D_FF = 1024
SWIGLU_LIMIT = 7.0
SWIGLU_ALPHA = 1.702
LN_EPS = 1e-5
RMS_EPS = 1e-6
LOG2_E = 1.4426950408889634

V7X_SUBLANES = 8
V7X_LANES = 128
V7X_VMEM_BYTES = 64 * 1024 * 1024
MIB = 1024 * 1024

TM = 512
TQ = 512
TK = 1024
RT = 256
EBLK = 512
ESUB = 128
CH = V7X_SUBLANES
SLOTS = 1280
V_SLOT = 80
HEAD_SLOT = V7X_LANES

assert SLOTS >= RT * TOP_K + N_EXPERTS * (CH - 1) and SLOTS % CH == 0


class Dims(NamedTuple):
    depth: int
    batch: int
    seq: int
    dec_batch: int
    dec_seq: int
    past: int

    @property
    def n_prompt(self):
        return self.batch * self.seq

    @property
    def n_sample(self):
        return self.dec_batch * self.dec_seq

    @property
    def n_tok(self):
        return self.n_prompt + self.n_sample

    @property
    def n_rt(self):
        return self.n_tok // RT

    @property
    def n_eblk(self):
        chunks = self.n_tok * TOP_K // CH + self.n_rt * N_EXPERTS + N_EXPERTS * (EBLK // CH - 1)
        return -(-chunks // (EBLK // CH))


def _cparams(sem, nbytes):
    return pltpu.CompilerParams(dimension_semantics=sem,
                                vmem_limit_bytes=min(nbytes, V7X_VMEM_BYTES - 8 * MIB))


def _dot(a, b):
    return jnp.dot(a, b, preferred_element_type=F32)


def _dot_nt(a, b):
    return lax.dot_general(a, b, (((1,), (1,)), ((), ())), preferred_element_type=F32)


def _layer_norm(v, g, b):
    mu = jnp.mean(v, axis=-1, keepdims=True)
    vc = v - mu
    var = jnp.mean(vc * vc, axis=-1, keepdims=True)
    return vc * lax.rsqrt(var + LN_EPS) * g + b


def _rms_norm(v, g):
    return v * lax.rsqrt(jnp.mean(v * v, axis=-1, keepdims=True) + RMS_EPS) * g


def _inproj_kernel(xp_ref, xs_ref, win_ref, tqa_ref, tqb_ref, tabk_ref, qg_ref, kvg_ref, wqa_ref, wqb_ref, wkx_ref,
                   ekr_ref, wvt_ref, vone_ref, *rest, n_pt):
    ux_ref, ug_ref, cp_ref, cs_ref, krp_ref, krs_ref, qx_ref, kx_ref, vt_ref = rest[-9:]
    is_sample = pl.program_id(0) >= n_pt
    x = jnp.where(is_sample, xs_ref[...], xp_ref[...])
    u = _dot(x.astype(BF16), win_ref[...])
    o1, o2 = LRU_WIDTH, 2 * LRU_WIDTH
    o3 = o2 + Q_RANK
    o4 = o3 + KV_RANK
    ux_ref[...] = u[:, :o1]
    ug_ref[...] = u[:, o1:o2]
    c_new = _rms_norm(u[:, o3:o4], kvg_ref[...])
    cb = c_new.astype(BF16)
    tk = tabk_ref[...]
    kr = (u[:, o4:o4 + ROPE_DIM] * tk[:, :ROPE_DIM]
          + u[:, o4 + ROPE_DIM:o4 + 2 * ROPE_DIM] * tk[:, ROPE_DIM:])
    cs_ref[0] = c_new
    krs_ref[0] = kr
    kn = _dot(cb, wkx_ref[...])
    kr_slot = _dot(kr.astype(BF16), ekr_ref[...])
    vt_ref[...] = (_dot_nt(wvt_ref[...], cb) + vone_ref[...]).astype(BF16)
    qn = _rms_norm(u[:, o2:o3], qg_ref[...]).astype(BF16)
    qa = _dot(qn, wqa_ref[...])
    qb = _dot(qn, wqb_ref[...])
    ta = tqa_ref[...]
    tb = tqb_ref[...]
    for h in range(MLA_HEADS):
        hs = slice(h * HEAD_SLOT, (h + 1) * HEAD_SLOT)
        kx_ref[:, hs] = (kn[:, hs] + kr_slot).astype(BF16)
        qx_ref[:, hs] = (qa[:, hs] * ta + qb[:, hs] * tb).astype(BF16)

    @pl.when(jnp.logical_not(is_sample))
    def _():
        cp_ref[0] = c_new
        krp_ref[0] = kr


def _x_specs(x, n_pt):
    if isinstance(x, tuple):
        return list(x), [pl.BlockSpec((TM, D_MODEL), lambda i: (jnp.minimum(i, n_pt - 1), 0)),
                         pl.BlockSpec((TM, D_MODEL), lambda i: (0, 0))]
    return [x, x], [pl.BlockSpec((TM, D_MODEL), lambda i: (i, 0)), pl.BlockSpec((TM, D_MODEL), lambda i: (n_pt, 0))]


def _inproj(x, lw, tabs, caches, layer, dims):
    n = dims.n_tok
    n_pt = dims.n_prompt // TM
    x_args, x_specs = _x_specs(x, n_pt)
    tps = dims.seq // TM
    tab_map = lambda i: (jnp.where(i < n_pt, i % tps, tps), 0)
    row = lambda w: pl.BlockSpec((TM, w), lambda i: (i, 0))
    full = lambda a: pl.BlockSpec(a.shape, lambda i: (0,) * a.ndim)
    tab = lambda a: pl.BlockSpec((TM, a.shape[1]), tab_map)
    weights = [lw["qg"], lw["kvg"], lw["wqa"], lw["wqb"], lw["wkx"], lw["ekr"], lw["wvt"], lw["vone"]]
    hw = MLA_HEADS * HEAD_SLOT
    args = x_args + [lw["win"], tabs["tqa"], tabs["tqb"], tabs["tabk"]] + weights
    in_specs = (x_specs + [full(lw["win"]), tab(tabs["tqa"]), tab(tabs["tqb"]), tab(tabs["tabk"])]
                + [full(w) for w in weights])
    aliases = {}
    if caches is not None:
        aliases = {len(args) + k: 2 + k for k in range(4)}
        args = args + list(caches)
        in_specs = in_specs + [pl.BlockSpec(memory_space=pl.ANY)] * 4
    p_map = lambda i: (layer, jnp.minimum(i, n_pt - 1), 0)
    s_map = lambda i: (layer, 0, 0)
    return pl.pallas_call(
        functools.partial(_inproj_kernel, n_pt=n_pt),
        grid=(n // TM,),
        in_specs=in_specs,
        out_specs=[row(LRU_WIDTH), row(LRU_WIDTH),
                   pl.BlockSpec((1, TM, KV_RANK), p_map), pl.BlockSpec((1, TM, KV_RANK), s_map),
                   pl.BlockSpec((1, TM, ROPE_DIM), p_map), pl.BlockSpec((1, TM, ROPE_DIM), s_map),
                   row(hw), row(hw), pl.BlockSpec((MLA_HEADS * V_SLOT, TM), lambda i: (0, i))],
        out_shape=[jax.ShapeDtypeStruct((n, LRU_WIDTH), F32), jax.ShapeDtypeStruct((n, LRU_WIDTH), F32),
                   jax.ShapeDtypeStruct((dims.depth, dims.n_prompt, KV_RANK), F32),
                   jax.ShapeDtypeStruct((dims.depth, dims.n_sample, KV_RANK), F32),
                   jax.ShapeDtypeStruct((dims.depth, dims.n_prompt, ROPE_DIM), F32),
                   jax.ShapeDtypeStruct((dims.depth, dims.n_sample, ROPE_DIM), F32),
                   jax.ShapeDtypeStruct((n, hw), BF16), jax.ShapeDtypeStruct((n, hw), BF16),
                   jax.ShapeDtypeStruct((MLA_HEADS * V_SLOT, n), BF16)],
        input_output_aliases=aliases,
        compiler_params=_cparams(("arbitrary",), 48 * MIB),
        name="inproj",
    )(*args)


def _linear_scan(a, b, h0):
    n = a.shape[0]
    sub = lax.broadcasted_iota(I32, a.shape, 0) % V7X_SUBLANES
    d = 1
    while d < V7X_SUBLANES:
        keep = sub >= d
        a_sh = jnp.where(keep, pltpu.roll(a, d, 0), 1.0)
        b_sh = jnp.where(keep, pltpu.roll(b, d, 0), 0.0)
        b = b + a * b_sh
        a = a * a_sh
        d *= 2
    carry = h0
    groups = []
    for g in range(n // V7X_SUBLANES):
        rows = slice(g * V7X_SUBLANES, (g + 1) * V7X_SUBLANES)
        hg = b[rows] + a[rows] * carry
        groups.append(hg)
        carry = hg[V7X_SUBLANES - 1:V7X_SUBLANES]
    return jnp.concatenate(groups, axis=0)


def _sigmoid(v):
    return 0.5 + 0.5 * jnp.tanh(0.5 * v)


def _gelu_tanh(v):
    return 0.5 * v * (1.0 + jnp.tanh(0.7978845608028654 * (v + 0.044715 * v * v * v)))


def _lru_kernel(ux_ref, ug_ref, st0_ref, cw_ref, cb_ref, wrg_ref, brg_ref, wig_ref, big_ref,
                lam_ref, out_ref, st_ref, xs_sc, carry_sc, *, seg, tps):
    t = pl.program_id(0) % tps
    lo = V7X_SUBLANES - (CONV_W - 1)

    @pl.when(t == 0)
    def _():
        carry_sc[...] = st0_ref[0]

    st_ref[0] = jnp.zeros(st_ref.shape[1:], F32)

    xs_sc[lo:V7X_SUBLANES, :] = carry_sc[lo:V7X_SUBLANES, :]
    xs_sc[V7X_SUBLANES:V7X_SUBLANES + seg, :] = ux_ref[...]
    cw = cw_ref[...]
    xc = cb_ref[...] + xs_sc[lo:lo + seg, :] * cw[0:1, :]
    for k in range(1, CONV_W):
        xc = xc + xs_sc[lo + k:lo + k + seg, :] * cw[k:k + 1, :]
    new_conv = xs_sc[seg + lo:seg + V7X_SUBLANES, :]
    carry_sc[lo:V7X_SUBLANES, :] = new_conv
    st_ref[0, lo:V7X_SUBLANES, :] = new_conv

    xcb = xc.astype(BF16)
    half = LRU_WIDTH // 2
    for g in range(2):
        cs = slice(g * half, (g + 1) * half)
        xg = xc[:, cs]
        r = _sigmoid(_dot(xcb[:, cs], wrg_ref[g]) + brg_ref[:, cs])
        ig = _sigmoid(_dot(xcb[:, cs], wig_ref[g]) + big_ref[:, cs])
        nl = -lam_ref[:, cs]
        softplus = jnp.maximum(nl, 0.0) + jnp.log1p(jnp.exp(-jnp.abs(nl)))
        log_a = -LRU_C * r * softplus
        a = jnp.exp(log_a)
        gap = 1.0 - a * a
        bt = jnp.where(gap > 0.0, gap * lax.rsqrt(gap), 0.0) * ig * xg
        h = _linear_scan(a, bt, carry_sc[0:1, cs])
        h_last = h[seg - 1:seg, :]
        carry_sc[0:1, cs] = h_last
        st_ref[0, 0:1, cs] = h_last
        out_ref[:, cs] = (_gelu_tanh(ug_ref[:, cs]) * h).astype(BF16)


def _lru_state(conv, h):
    lo = V7X_SUBLANES - (CONV_W - 1)
    return jnp.concatenate([h[:, None, :], jnp.zeros((h.shape[0], lo - 1, h.shape[1]), F32), conv], axis=1)


def _lru(ux, ug, st0, lw, prev_out, *, seg, n_seg, tps, row0):
    n = ux.shape[0]
    blk0 = row0 // seg
    row = pl.BlockSpec((seg, LRU_WIDTH), lambda i: (blk0 + i, 0))
    full = lambda a: pl.BlockSpec(a.shape, lambda i: (0,) * a.ndim)
    n_seq = n_seg // tps
    weights = [lw["conv_w"], lw["conv_b"], lw["wrg"], lw["b_rg"], lw["wig"], lw["b_ig"], lw["lam"]]
    in_specs = [row, row,
                pl.BlockSpec((1, V7X_SUBLANES, LRU_WIDTH), lambda i: (i // tps, 0, 0))] + [full(w) for w in weights]
    args = [ux, ug, st0] + weights
    aliases = {}
    if prev_out is not None:
        in_specs.append(pl.BlockSpec(memory_space=pl.ANY))
        args.append(prev_out)
        aliases = {len(args) - 1: 0}

    def body(*refs):
        refs = list(refs)
        if prev_out is not None:
            del refs[len(args) - 1]
        _lru_kernel(*refs, seg=seg, tps=tps)

    return pl.pallas_call(
        body,
        grid=(n_seg,),
        in_specs=in_specs,
        out_specs=[row, pl.BlockSpec((1, V7X_SUBLANES, LRU_WIDTH), lambda i: (i // tps, 0, 0))],
        out_shape=[jax.ShapeDtypeStruct((n, LRU_WIDTH), BF16),
                   jax.ShapeDtypeStruct((n_seq, V7X_SUBLANES, LRU_WIDTH), F32)],
        scratch_shapes=[pltpu.VMEM((seg + V7X_SUBLANES, LRU_WIDTH), F32),
                        pltpu.VMEM((V7X_SUBLANES, LRU_WIDTH), F32)],
        input_output_aliases=aliases,
        compiler_params=_cparams(("arbitrary",), 40 * MIB),
        name="lru_seg%d" % seg,
    )(*args)


def _attn_prompt_kernel(qi_ref, kj_ref, mk_ref, qx_ref, kx_ref, vt_ref, o_ref, m_sc, acc_sc):
    pair = pl.program_id(1)
    i = qi_ref[pair]
    j = kj_ref[pair]
    flag = mk_ref[pair]

    @pl.when(j == 0)
    def _():
        m_sc[...] = jnp.full(m_sc.shape, -jnp.inf, F32)
        acc_sc[...] = jnp.zeros(acc_sc.shape, F32)

    def step(masked, n_keys):
        if masked:
            kpos = lax.broadcasted_iota(I32, (n_keys, TQ), 0) + j * TK
            qpos = lax.broadcasted_iota(I32, (n_keys, TQ), 1) + i * TQ
            visible = kpos < (qpos // CHUNK + 1) * CHUNK

        def scores(h):
            hs = slice(h * HEAD_SLOT, (h + 1) * HEAD_SLOT)
            return _dot_nt(kx_ref[:n_keys, hs], qx_ref[:, hs])

        pending = [scores(0), scores(1)]
        for h in range(MLA_HEADS):
            s = pending.pop(0)
            if h + 2 < MLA_HEADS:
                pending.append(scores(h + 2))
            if masked:
                s = jnp.where(visible, s, -jnp.inf)
            m_prev = m_sc[h]
            m_new = jnp.maximum(m_prev, jnp.max(s, axis=0, keepdims=True))
            alpha = jnp.exp2(m_prev - m_new)
            p = jnp.exp2(s - m_new).astype(BF16)
            acc_sc[h] = alpha * acc_sc[h] + _dot(vt_ref[h * V_SLOT:(h + 1) * V_SLOT, :n_keys], p)
            m_sc[h] = m_new

    @pl.when((flag & 5) == 0)
    def _():
        step(False, TK)

    @pl.when((flag & 5) == 1)
    def _():
        step(True, TK)

    if TK > TQ:
        @pl.when((flag & 5) == 5)
        def _():
            step(True, TQ)

    @pl.when((flag & 2) == 2)
    def _():
        o_t = jnp.concatenate([acc_sc[h, :V_DIM, :] / acc_sc[h, V_DIM:V_DIM + 1, :] for h in range(MLA_HEADS)],
                              axis=0)
        o_ref[...] = o_t.T.astype(BF16)


def _attn_prompt(qx, kx, vt, dims):
    n = dims.n_tok
    nq = dims.seq // TQ
    nk = dims.seq // TK
    pairs = []
    for i in range(nq):
        n_kv = -(-(i + 1) * TQ // TK)
        for j in range(n_kv):
            short = TK > TQ and (i + 1) * TQ <= j * TK + TQ
            pairs.append((i, j, int((j + 1) * TK > i * TQ) | (2 if j == n_kv - 1 else 0) | (4 if short else 0)))
    qi = jnp.asarray([p[0] for p in pairs], I32)
    kj = jnp.asarray([p[1] for p in pairs], I32)
    mk = jnp.asarray([p[2] for p in pairs], I32)
    hw = MLA_HEADS * HEAD_SLOT
    return pl.pallas_call(
        _attn_prompt_kernel,
        grid_spec=pltpu.PrefetchScalarGridSpec(
            num_scalar_prefetch=3,
            grid=(dims.batch, len(pairs)),
            in_specs=[pl.BlockSpec((TQ, hw), lambda b, p, qi, kj, mk: (b * nq + qi[p], 0)),
                      pl.BlockSpec((TK, hw), lambda b, p, qi, kj, mk: (b * nk + kj[p], 0)),
                      pl.BlockSpec((MLA_HEADS * V_SLOT, TK), lambda b, p, qi, kj, mk: (0, b * nk + kj[p]))],
            out_specs=pl.BlockSpec((TQ, MLA_HEADS * V_DIM), lambda b, p, qi, kj, mk: (b * nq + qi[p], 0)),
            scratch_shapes=[pltpu.VMEM((MLA_HEADS, 1, TQ), F32), pltpu.VMEM((MLA_HEADS, V_SLOT, TQ), F32)],
        ),
        out_shape=jax.ShapeDtypeStruct((n, MLA_HEADS * V_DIM), BF16),
        compiler_params=_cparams(("parallel", "arbitrary"), 48 * MIB),
        name="attn_prompt",
    )(qi, kj, mk, qx, kx, vt)


def _attn_sample_kernel(qx_ref, pk_ref, pkr_ref, c_ref, kr_ref, wabs_ref, ekr_ref, wuv_ref, prev_ref, o_ref,
                        *, dec_seq):
    del prev_ref
    slots = [qx_ref[:, h * HEAD_SLOT:(h + 1) * HEAD_SLOT] for h in range(MLA_HEADS)]
    q_slot = jnp.concatenate(slots, axis=0)
    q_lat = jnp.concatenate([_dot(slots[h], wabs_ref[h]) for h in range(MLA_HEADS)], axis=0).astype(BF16)
    ekr = ekr_ref[...]
    pk = pk_ref[0, 0].astype(BF16)
    pkr = _dot(pkr_ref[0, 0].astype(BF16), ekr).astype(BF16)
    k = c_ref[0].astype(BF16)
    kr = _dot(kr_ref[0].astype(BF16), ekr).astype(BF16)
    s_past = _dot_nt(q_lat, pk) + _dot_nt(q_slot, pkr)
    s_new = _dot_nt(q_lat, k) + _dot_nt(q_slot, kr)
    m = jnp.maximum(jnp.max(s_past, axis=1, keepdims=True), jnp.max(s_new, axis=1, keepdims=True))
    p_past = jnp.exp2(s_past - m)
    p_new = jnp.exp2(s_new - m)
    denom = jnp.sum(p_past, axis=1, keepdims=True) + jnp.sum(p_new, axis=1, keepdims=True)
    o_lat = (_dot(p_past.astype(BF16), pk) + _dot(p_new.astype(BF16), k)) / denom
    for h in range(MLA_HEADS):
        o_h = o_lat[h * dec_seq:(h + 1) * dec_seq, :].astype(BF16)
        o_ref[:, h * V_DIM:(h + 1) * V_DIM] = _dot(o_h, wuv_ref[h]).astype(BF16)


def _attn_sample(qx, cache_ckv, cache_krope, c_s, kr_s, lw, attn_prev, layer, dims):
    ds = dims.dec_seq
    blk0 = dims.n_prompt // ds
    rmap = lambda s: (blk0 + s, 0)
    smap = lambda s: (layer, s, 0)
    past = cache_ckv.shape[2]
    full = lambda a: pl.BlockSpec(a.shape, lambda s: (0,) * a.ndim)
    return pl.pallas_call(
        functools.partial(_attn_sample_kernel, dec_seq=ds),
        grid=(dims.dec_batch,),
        in_specs=[pl.BlockSpec((ds, MLA_HEADS * HEAD_SLOT), rmap),
                  pl.BlockSpec((1, 1, past, KV_RANK), lambda s: (layer, s, 0, 0)),
                  pl.BlockSpec((1, 1, past, ROPE_DIM), lambda s: (layer, s, 0, 0)),
                  pl.BlockSpec((1, ds, KV_RANK), smap), pl.BlockSpec((1, ds, ROPE_DIM), smap),
                  full(lw["wabs"]), full(lw["ekr"]), full(lw["wuv"]),
                  pl.BlockSpec(memory_space=pl.ANY)],
        out_specs=pl.BlockSpec((ds, MLA_HEADS * V_DIM), rmap),
        out_shape=jax.ShapeDtypeStruct(attn_prev.shape, attn_prev.dtype),
        input_output_aliases={8: 0},
        compiler_params=_cparams(("parallel",), 48 * MIB),
        name="attn_sample",
    )(qx, cache_ckv, cache_krope, c_s, kr_s, lw["wabs"], lw["ekr"], lw["wuv"], attn_prev)


def _outproj_kernel(xp_ref, xs_ref, lru_ref, attn_ref, wo_ref, g_ref, b_ref, wr_ref, br_ref, tri_ref, low_ref,
                    o_ref, row_ref, col_ref, cnt_ref, *, dn_alpha, n_pt):
    x = jnp.where(pl.program_id(0) >= n_pt, xs_ref[...], xp_ref[...])
    mix = _dot(lru_ref[...], wo_ref[:LRU_WIDTH, :]) + _dot(attn_ref[...], wo_ref[LRU_WIDTH:, :])
    x1 = _layer_norm(dn_alpha * x + mix, g_ref[...], b_ref[...])
    o_ref[...] = x1
    routed = _route_block(x1, wr_ref[...], br_ref[...], tri_ref[...], low_ref[...])
    for r, (info, cnt) in enumerate(routed):
        row_ref[r] = info
        pad = jnp.zeros((V7X_LANES - 2 * TOP_K, RT), F32)
        col_ref[r] = jnp.concatenate([info, pad], axis=0).T
        cnt_ref[r] = jnp.broadcast_to(cnt, cnt_ref.shape[1:])


def _outproj(x, lru_out, attn, lw, tri, low, dims, dn_alpha):
    n = dims.n_tok
    n_pt = dims.n_prompt // TM
    per = TM // RT
    x_args, x_specs = _x_specs(x, n_pt)
    row = lambda w: pl.BlockSpec((TM, w), lambda i: (i, 0))
    full = lambda a: pl.BlockSpec(a.shape, lambda i: (0,) * a.ndim)
    consts = [lw["wo"], lw["ln1_g"], lw["ln1_b"], lw["wr_t"], lw["br"], tri, low]
    return pl.pallas_call(
        functools.partial(_outproj_kernel, dn_alpha=dn_alpha, n_pt=n_pt),
        grid=(n // TM,),
        in_specs=x_specs + [row(LRU_WIDTH), row(MLA_HEADS * V_DIM)] + [full(c) for c in consts],
        out_specs=[row(D_MODEL),
                   pl.BlockSpec((per, 2 * TOP_K, RT), lambda i: (i, 0, 0)),
                   pl.BlockSpec((per, RT, V7X_LANES), lambda i: (i, 0, 0)),
                   pl.BlockSpec((per, N_EXPERTS, V7X_LANES), lambda i: (i, 0, 0))],
        out_shape=[jax.ShapeDtypeStruct((n, D_MODEL), F32),
                   jax.ShapeDtypeStruct((dims.n_rt, 2 * TOP_K, RT), F32),
                   jax.ShapeDtypeStruct((dims.n_rt, RT, V7X_LANES), F32),
                   jax.ShapeDtypeStruct((dims.n_rt, N_EXPERTS, V7X_LANES), F32)],
        compiler_params=_cparams(("parallel",), 40 * MIB),
        name="outproj",
    )(*x_args, lru_out, attn, *consts)


def _split_bf16(v):
    hi = v.astype(BF16)
    return hi, (v - hi.astype(F32)).astype(BF16)


def _route_block(x, wr, br, tri, low):
    x_hi, x_lo = _split_bf16(x)
    w_hi, w_lo = _split_bf16(wr)
    logits = _dot_nt(w_hi, x_hi) + (_dot_nt(w_hi, x_lo) + _dot_nt(w_lo, x_hi)) + br
    eidx = lax.broadcasted_iota(I32, logits.shape, 0)
    work = logits
    onehots, vals = [], []
    for _ in range(TOP_K):
        m = jnp.max(work, axis=0, keepdims=True)
        first = jnp.min(jnp.where(work == m, eidx, N_EXPERTS), axis=0, keepdims=True)
        oh = eidx == first
        onehots.append(oh)
        vals.append(m)
        work = jnp.where(oh, -jnp.inf, work)
    ex = [jnp.exp(v - vals[0]) for v in vals]
    denom = ex[0] + ex[1] + ex[2] + ex[3]
    gates = [e / denom for e in ex]
    self32 = (onehots[0] | onehots[1] | onehots[2] | onehots[3]).astype(F32)
    out = []
    for r in range(x.shape[0] // RT):
        cols = slice(r * RT, (r + 1) * RT)
        sel_r = self32[:, cols]
        rank = _dot(sel_r.astype(BF16), tri)
        cnt = jnp.sum(sel_r, axis=1, keepdims=True)
        padded = jnp.floor((cnt + (CH - 1)) * (1.0 / CH)) * CH
        lstart = _dot(low, jnp.broadcast_to(padded, (N_EXPERTS, V7X_LANES)).astype(BF16))[:, 0:1]
        slotmat = lstart + rank
        rows = [jnp.sum(jnp.where(oh[:, cols], slotmat, 0.0), axis=0, keepdims=True) for oh in onehots]
        rows += [g[:, cols] for g in gates]
        out.append((jnp.concatenate(rows, axis=0), cnt))
    return out


def _routing_tables(cnt, dims):
    cpb = EBLK // CH
    nch = (cnt.astype(I32) + (CH - 1)) // CH
    tot = jnp.sum(nch, axis=0)
    nblk = (tot + cpb - 1) // cpb
    blk_end = jnp.cumsum(nblk)
    base = (blk_end - nblk) * cpb
    gstart = base[None, :] + jnp.cumsum(nch, axis=0) - nch
    blk_ids = jnp.arange(dims.n_eblk, dtype=I32)
    blk_e = jnp.minimum(jnp.sum((blk_end[None, :] <= blk_ids[:, None]).astype(I32), axis=1), N_EXPERTS - 1)
    e_ids = jnp.arange(N_EXPERTS, dtype=I32)
    owner = (blk_e[:, None] == e_ids[None, :]).astype(I32)
    first_blk = jnp.sum(owner * (blk_end - nblk)[None, :], axis=1)
    blk_valid = jnp.clip(jnp.sum(owner * tot[None, :], axis=1) * CH - (blk_ids - first_blk) * EBLK,
                         0, EBLK).astype(I32)
    later_used = (e_ids[None, :] > e_ids[:, None]) & (nblk[None, :] > 0)
    next_e = jnp.min(jnp.where(later_used, e_ids[None, :], N_EXPERTS), axis=1)
    next_e = jnp.where(next_e < N_EXPERTS, next_e, -1).astype(I32)
    return dict(nch=nch.reshape(-1), gstart=gstart.astype(I32).reshape(-1), ntot=jnp.sum(nch, axis=1).astype(I32),
                next_e=next_e, blk_valid=blk_valid,
                tail_start=(base + tot).astype(I32), tail_n=(nblk * cpb - tot).astype(I32),
                blk_e=blk_e, nb_used=blk_end[-1:].astype(I32))


def _group_chunks(nch_ref, gst_ref, tile, visit):
    l0 = jnp.int32(0)
    for e in range(N_EXPERTS):
        n = nch_ref[tile * N_EXPERTS + e]
        g0 = gst_ref[tile * N_EXPERTS + e]

        def body(c, carry, l0=l0, g0=g0):
            visit(l0 + c, g0 + c)
            return carry

        lax.fori_loop(0, n, body, 0)
        l0 = l0 + n
    return l0


def _chunk(ref, c):
    if isinstance(c, int):
        return ref.at[pl.ds(c * CH, CH)]
    return ref.at[pl.ds(pl.multiple_of(c * CH, CH), CH)]


def _wait_chunks(vmem_ref, hbm_ref, sem, count):
    @pl.when(count > 0)
    def _():
        rows = pl.ds(0, count * CH)
        pltpu.make_async_copy(vmem_ref.at[rows], hbm_ref.at[rows], sem).wait()


def _dispatch_kernel(nch_ref, gst_ref, ntot_ref, tls_ref, tln_ref, x_ref, row_ref, xs_ref, loc_sc, zero_sc, sems):
    i = pl.program_id(0)
    last = pl.num_programs(0) - 1
    slot = i % 2

    @pl.when(i >= 2)
    def _():
        _wait_chunks(loc_sc.at[slot], xs_ref, sems.at[slot], ntot_ref[jnp.maximum(i - 2, 0)])

    slots = row_ref[0]
    srow = lax.broadcasted_iota(I32, (SLOTS, RT), 0).astype(F32)
    perm = jnp.zeros((SLOTS, RT), F32)
    for k in range(TOP_K):
        perm = jnp.where(srow == slots[k:k + 1], 1.0, perm)
    loc_sc[slot] = _dot(perm.astype(BF16), x_ref[...].astype(BF16))
    loc = loc_sc.at[slot]
    _group_chunks(nch_ref, gst_ref, i,
                  lambda lc, gc: pltpu.make_async_copy(_chunk(loc, lc), _chunk(xs_ref, gc), sems.at[slot]).start())

    @pl.when(i == last)
    def _():
        _wait_chunks(loc, xs_ref, sems.at[slot], ntot_ref[i])

        @pl.when(i >= 1)
        def _():
            _wait_chunks(loc_sc.at[1 - slot], xs_ref, sems.at[1 - slot], ntot_ref[jnp.maximum(i - 1, 0)])

        zero_sc[...] = jnp.zeros(zero_sc.shape, F32)
        n_tail = jnp.int32(0)
        for e in range(N_EXPERTS):
            n = tln_ref[e]
            g0 = tls_ref[e]

            def body(c, carry, g0=g0):
                pltpu.make_async_copy(zero_sc, _chunk(xs_ref, g0 + c), sems.at[0]).start()
                return carry

            lax.fori_loop(0, n, body, 0)
            n_tail = n_tail + n

        def wait_tail(c, carry):
            pltpu.make_async_copy(zero_sc, _chunk(xs_ref, 0), sems.at[0]).wait()
            return carry

        lax.fori_loop(0, n_tail, wait_tail, 0)


def _dispatch(x1, rowinfo, tabs, dims):
    rows = dims.n_eblk * EBLK
    return pl.pallas_call(
        _dispatch_kernel,
        grid_spec=pltpu.PrefetchScalarGridSpec(
            num_scalar_prefetch=5,
            grid=(dims.n_rt,),
            in_specs=[pl.BlockSpec((RT, D_MODEL), lambda i, *_: (i, 0)),
                      pl.BlockSpec((1, 2 * TOP_K, RT), lambda i, *_: (i, 0, 0))],
            out_specs=pl.BlockSpec(memory_space=pl.ANY),
            scratch_shapes=[pltpu.VMEM((2, SLOTS, D_MODEL), F32), pltpu.VMEM((CH, D_MODEL), F32),
                            pltpu.SemaphoreType.DMA((2,))],
        ),
        out_shape=jax.ShapeDtypeStruct((rows, D_MODEL), F32),
        compiler_params=_cparams(("arbitrary",), 40 * MIB),
        name="dispatch",
    )(tabs["nch"], tabs["gstart"], tabs["ntot"], tabs["tail_start"], tabs["tail_n"], x1, rowinfo)


def _expert_kernel(be_ref, nb_ref, nxt_ref, bv_ref, xs_ref, wgu_hbm, bgu_ref, wd_hbm, bd_ref, y_ref,
                   wgu_st, wd_st, wgu_sc, wd_sc, sems, *, layer):
    b = pl.program_id(0)

    def copies(e):
        return (pltpu.make_async_copy(wgu_hbm.at[layer, e], wgu_st, sems.at[0]),
                pltpu.make_async_copy(wd_hbm.at[layer, e], wd_st, sems.at[1]))

    def mlp(n_rows):
        rows = slice(0, n_rows)
        h = _dot(xs_ref[rows, :].astype(BF16), wgu_sc[...]) + bgu_ref[0, 0]
        hg = jnp.minimum(h[:, :D_FF], SWIGLU_LIMIT)
        hl = jnp.clip(h[:, D_FF:], -SWIGLU_LIMIT, SWIGLU_LIMIT)
        act = hg * jax.nn.sigmoid(SWIGLU_ALPHA * hg) * (hl + 1.0)
        y_ref[rows, :] = _dot(act.astype(BF16), wd_sc[...]) + bd_ref[0, 0]

    @pl.when(b < nb_ref[0])
    def _():
        e = be_ref[b]

        @pl.when(b == 0)
        def _():
            for c in copies(e):
                c.start()

        @pl.when((b == 0) | (e != be_ref[jnp.maximum(b - 1, 0)]))
        def _():
            for c in copies(e):
                c.wait()
            wgu_sc[...] = wgu_st[...].astype(BF16)
            wd_sc[...] = wd_st[...].astype(BF16)
            nxt = nxt_ref[e]

            @pl.when(nxt >= 0)
            def _():
                for c in copies(nxt):
                    c.start()

        valid = bv_ref[b]
        for n_rows in range(ESUB, EBLK + 1, ESUB):
            @pl.when((valid > n_rows - ESUB) & (valid <= n_rows))
            def _(n_rows=n_rows):
                mlp(n_rows)


def _experts(xs, w_gu, b_gu, w_down, b_down, tabs, layer, dims):
    blk = lambda b, be, nb, *_: jnp.minimum(b, nb[0] - 1)
    wmap = lambda b, be, nb, *_: (layer, be[blk(b, be, nb)], 0, 0)
    row = pl.BlockSpec((EBLK, D_MODEL), lambda b, be, nb, *_: (blk(b, be, nb), 0))
    return pl.pallas_call(
        functools.partial(_expert_kernel, layer=layer),
        grid_spec=pltpu.PrefetchScalarGridSpec(
            num_scalar_prefetch=4,
            grid=(dims.n_eblk,),
            in_specs=[row, pl.BlockSpec(memory_space=pl.ANY), pl.BlockSpec((1, 1, 1, 2 * D_FF), wmap),
                      pl.BlockSpec(memory_space=pl.ANY), pl.BlockSpec((1, 1, 1, D_MODEL), wmap)],
            out_specs=row,
            scratch_shapes=[pltpu.VMEM((D_MODEL, 2 * D_FF), F32), pltpu.VMEM((D_FF, D_MODEL), F32),
                            pltpu.VMEM((D_MODEL, 2 * D_FF), BF16), pltpu.VMEM((D_FF, D_MODEL), BF16),
                            pltpu.SemaphoreType.DMA((2,))],
        ),
        out_shape=jax.ShapeDtypeStruct(xs.shape, F32),
        compiler_params=_cparams(("arbitrary",), 54 * MIB),
        name="experts",
    )(tabs["blk_e"], tabs["nb_used"], tabs["next_e"], tabs["blk_valid"], xs, w_gu, b_gu, w_down, b_down)


def _combine_kernel(nch_ref, gst_ref, ntot_ref, x_ref, col_ref, yb_ref, g_ref, b_ref, *rest, dn_alpha, n_prt):
    o_refs, (loc_sc, sems) = rest[:-2], rest[-2:]
    i = pl.program_id(0)
    slot = i % 2

    def fetch(tile, sl):
        loc = loc_sc.at[sl]
        total = _group_chunks(
            nch_ref, gst_ref, tile,
            lambda lc, gc: pltpu.make_async_copy(_chunk(yb_ref, gc), _chunk(loc, lc), sems.at[sl]).start())

        def zero(c, carry):
            _chunk(loc, c)[...] = jnp.zeros((CH, D_MODEL), F32)
            return carry

        lax.fori_loop(total, SLOTS // CH, zero, 0)

    @pl.when(i == 0)
    def _():
        fetch(i, slot)

    @pl.when(i + 1 < pl.num_programs(0))
    def _():
        fetch(i + 1, 1 - slot)

    info = col_ref[0]
    scol = lax.broadcasted_iota(I32, (RT, SLOTS), 1).astype(F32)
    gmat = jnp.zeros((RT, SLOTS), F32)
    for k in reversed(range(TOP_K)):
        gmat = jnp.where(scol == info[:, k:k + 1], info[:, TOP_K + k:TOP_K + k + 1], gmat)

    _wait_chunks(loc_sc.at[slot], yb_ref, sems.at[slot], ntot_ref[i])
    y = _dot(gmat.astype(BF16), loc_sc[slot].astype(BF16))
    out = _layer_norm(dn_alpha * x_ref[...] + y, g_ref[...], b_ref[...])
    if n_prt is None:
        o_refs[0][...] = out
    else:
        @pl.when(i < n_prt)
        def _():
            o_refs[0][...] = out

        @pl.when(i >= n_prt)
        def _():
            o_refs[1][...] = out


def _combine(x1, colinfo, yb, g, b, tabs, dims, dn_alpha, split):
    n = dims.n_tok
    n_prt = dims.n_prompt // RT
    if split:
        out_specs = [pl.BlockSpec((RT, D_MODEL), lambda i, *_: (jnp.minimum(i, n_prt - 1), 0)),
                     pl.BlockSpec((RT, D_MODEL), lambda i, *_: (jnp.maximum(i - n_prt, 0), 0))]
        out_shape = [jax.ShapeDtypeStruct((dims.n_prompt, D_MODEL), F32),
                     jax.ShapeDtypeStruct((dims.n_sample, D_MODEL), F32)]
    else:
        out_specs = pl.BlockSpec((RT, D_MODEL), lambda i, *_: (i, 0))
        out_shape = jax.ShapeDtypeStruct((n, D_MODEL), F32)
    return pl.pallas_call(
        functools.partial(_combine_kernel, dn_alpha=dn_alpha, n_prt=n_prt if split else None),
        grid_spec=pltpu.PrefetchScalarGridSpec(
            num_scalar_prefetch=3,
            grid=(dims.n_rt,),
            in_specs=[pl.BlockSpec((RT, D_MODEL), lambda i, *_: (i, 0)),
                      pl.BlockSpec((1, RT, V7X_LANES), lambda i, *_: (i, 0, 0)),
                      pl.BlockSpec(memory_space=pl.ANY),
                      pl.BlockSpec(g.shape, lambda i, *_: (0, 0)), pl.BlockSpec(b.shape, lambda i, *_: (0, 0))],
            out_specs=out_specs,
            scratch_shapes=[pltpu.VMEM((2, SLOTS, D_MODEL), F32), pltpu.SemaphoreType.DMA((2,))],
        ),
        out_shape=out_shape,
        compiler_params=_cparams(("arbitrary",), 40 * MIB),
        name="combine",
    )(tabs["nch"], tabs["gstart"], tabs["ntot"], x1, colinfo, yb, g, b)


def _swap_halves(w, group):
    shp = w.shape
    w = w.reshape(shp[:-1] + (shp[-1] // group, 2, group // 2))
    return w[..., ::-1, :].reshape(shp)


def _block_diag(blocks, per):
    h, a, b = blocks.shape
    grouped = blocks.reshape(h // per, per, a, b)
    out = jnp.zeros((h // per, per * a, per * b), blocks.dtype)
    for p in range(per):
        out = out.at[:, p * a:(p + 1) * a, p * b:(p + 1) * b].set(grouped[:, p])
    return out


def _rope_tables(dims):
    half = ROPE_DIM // 2
    pos = jnp.concatenate([jnp.arange(dims.seq, dtype=I32),
                           jnp.tile(dims.past + jnp.arange(dims.dec_seq, dtype=I32), dims.dec_batch)]).astype(F32)
    freqs = ROPE_THETA ** (-2.0 * jnp.arange(half, dtype=F32) / ROPE_DIM)
    ang = pos[:, None] * freqs[None, :]
    cos, sin = jnp.cos(ang), jnp.sin(ang)
    cos2 = jnp.concatenate([cos, cos], axis=1)
    sin2 = jnp.concatenate([-sin, sin], axis=1)
    scale = (NOPE_DIM + ROPE_DIM) ** -0.5 * LOG2_E
    n = pos.shape[0]
    pad = jnp.zeros((n, HEAD_SLOT - NOPE_DIM - ROPE_DIM), F32)
    tqa = jnp.concatenate([jnp.full((n, NOPE_DIM), scale, F32), cos2 * scale, pad], axis=1)
    tqb = jnp.concatenate([jnp.zeros((n, NOPE_DIM), F32), sin2 * scale, pad], axis=1)
    return dict(tqa=tqa, tqb=tqb, tabk=jnp.concatenate([cos2, sin2], axis=1))


def _head_slots(nope, rope):
    r = nope.shape[0]
    pad = jnp.zeros((r, MLA_HEADS, HEAD_SLOT - NOPE_DIM - ROPE_DIM), nope.dtype)
    return jnp.concatenate([nope, rope, pad], axis=2).reshape(r, MLA_HEADS * HEAD_SLOT)


def _layer_weights(p, l):
    o4 = 2 * LRU_WIDTH + Q_RANK + KV_RANK
    w_in = p["w_in"][l]
    win = jnp.concatenate([w_in, _swap_halves(w_in[:, o4:], ROPE_DIM)], axis=1).astype(BF16)
    w_uq = p["w_uq"][l].reshape(Q_RANK, MLA_HEADS, NOPE_DIM + ROPE_DIM)
    uq_nope, uq_rope = w_uq[:, :, :NOPE_DIM], w_uq[:, :, NOPE_DIM:]
    w_uk = p["w_uk"][l]
    zero_rope = jnp.zeros((KV_RANK, MLA_HEADS, ROPE_DIM), F32)
    ekr = jnp.zeros((ROPE_DIM, HEAD_SLOT), F32).at[jnp.arange(ROPE_DIM), NOPE_DIM + jnp.arange(ROPE_DIM)].set(1.0)
    wabs = jnp.pad(jnp.transpose(w_uk, (1, 2, 0)), ((0, 0), (0, HEAD_SLOT - NOPE_DIM), (0, 0)))
    return dict(
        win=win,
        wqa=_head_slots(uq_nope, uq_rope).astype(BF16),
        wqb=_head_slots(jnp.zeros_like(uq_nope), _swap_halves(uq_rope, ROPE_DIM)).astype(BF16),
        wkx=_head_slots(w_uk, zero_rope).astype(BF16),
        ekr=ekr.astype(BF16),
        wvt=jnp.pad(jnp.transpose(p["w_uv"][l], (1, 2, 0)), ((0, 0), (0, V_SLOT - V_DIM), (0, 0))
                    ).reshape(MLA_HEADS * V_SLOT, KV_RANK).astype(BF16),
        vone=jnp.tile((jnp.arange(V_SLOT) == V_DIM).astype(F32), MLA_HEADS)[:, None],
        wabs=wabs.astype(BF16),
        wuv=jnp.transpose(p["w_uv"][l], (1, 0, 2)).astype(BF16),
        qg=p["q_norm_g"][l][None, :], kvg=p["kv_norm_g"][l][None, :],
        conv_w=p["conv_w"][l], conv_b=p["conv_b"][l][None, :],
        wrg=_block_diag(p["w_rg"][l], 4).astype(BF16), b_rg=p["b_rg"][l][None, :],
        wig=_block_diag(p["w_ig"][l], 4).astype(BF16), b_ig=p["b_ig"][l][None, :],
        lam=p["lru_lambda"][l][None, :],
        wo=p["w_o"][l].astype(BF16), ln1_g=p["ln1_g"][l][None, :], ln1_b=p["ln1_b"][l][None, :],
        wr_t=p["w_router"][l].T, br=p["b_router"][l][:, None],
        ln2_g=p["ln2_g"][l][None, :], ln2_b=p["ln2_b"][l][None, :],
    )


def _trunk(x_prompt, x_sample, cache_ckv, cache_krope, state_conv, state_lru, p):
    depth = p["w_in"].shape[0]
    dims = Dims(depth, x_prompt.shape[0], x_prompt.shape[1], x_sample.shape[0], x_sample.shape[1],
                cache_ckv.shape[2])
    assert dims.n_sample == TM and dims.seq % TM == 0 and dims.dec_seq == CHUNK and dims.n_tok % RT == 0
    dn_alpha = (2.0 * depth) ** 0.25
    x = (x_prompt.reshape(-1, D_MODEL), x_sample.reshape(-1, D_MODEL))
    rope_tabs = _rope_tables(dims)
    tri = (jnp.arange(RT)[:, None] < jnp.arange(RT)[None, :]).astype(BF16)
    low = (jnp.arange(N_EXPERTS)[None, :] < jnp.arange(N_EXPERTS)[:, None]).astype(BF16)
    b_gu = p["b_gu"][:, :, None, :]
    b_down = p["b_down"][:, :, None, :]
    zero_state = jnp.zeros((dims.batch, V7X_SUBLANES, LRU_WIDTH), F32)
    tps = dims.seq // TM
    n_p = dims.n_prompt
    lo = V7X_SUBLANES - (CONV_W - 1)
    outs = {k: [] for k in ("conv_p", "h_p", "conv_s", "h_s")}
    caches = None
    for l in range(depth):
        lw = _layer_weights(p, l)
        ux, ug, *caches, qx, kx, vt = _inproj(x, lw, rope_tabs, caches, l, dims)
        lru_out, st_p = _lru(ux, ug, zero_state, lw, None,
                             seg=TM, n_seg=dims.batch * tps, tps=tps, row0=0)
        lru_out, st_s = _lru(ux, ug, _lru_state(state_conv[l], state_lru[l]), lw, lru_out,
                             seg=dims.dec_seq, n_seg=dims.dec_batch, tps=1, row0=n_p)
        attn = _attn_prompt(qx, kx, vt, dims)
        attn = _attn_sample(qx, cache_ckv, cache_krope, caches[1], caches[3], lw, attn, l, dims)
        x1, rowinfo, colinfo, cnt = _outproj(x, lru_out, attn, lw, tri, low, dims, dn_alpha)
        tabs = _routing_tables(cnt[:, :, 0], dims)
        xs = _dispatch(x1, rowinfo, tabs, dims)
        yb = _experts(xs, p["w_gu"], b_gu, p["w_down"], b_down, tabs, l, dims)
        x = _combine(x1, colinfo, yb, lw["ln2_g"], lw["ln2_b"], tabs, dims, dn_alpha, split=l == depth - 1)
        outs["conv_p"].append(st_p[:, lo:, :])
        outs["h_p"].append(st_p[:, 0, :])
        outs["conv_s"].append(st_s[:, lo:, :])
        outs["h_s"].append(st_s[:, 0, :])
    st = {k: jnp.stack(v) for k, v in outs.items()}
    c_p, c_s, kr_p, kr_s = caches
    st["ckv_p"] = c_p.reshape(depth, dims.batch, dims.seq, KV_RANK)
    st["kr_p"] = kr_p.reshape(depth, dims.batch, dims.seq, ROPE_DIM)
    st["ckv_s"] = c_s.reshape(depth, dims.dec_batch, dims.dec_seq, KV_RANK)
    st["kr_s"] = kr_s.reshape(depth, dims.dec_batch, dims.dec_seq, ROPE_DIM)
    y_prompt = x[0].reshape(x_prompt.shape)
    y_sample = x[1].reshape(x_sample.shape)
    return (y_prompt, y_sample, st["ckv_p"], st["kr_p"], st["conv_p"], st["h_p"],
            st["ckv_s"], st["kr_s"], st["conv_s"], st["h_s"])


def kernel(x_prompt, x_sample, cache_ckv, cache_krope, state_conv, state_lru, w_in, conv_w, conv_b, w_rg, b_rg,
           w_ig, b_ig, lru_lambda, q_norm_g, w_uq, kv_norm_g, w_uk, w_uv, w_o, ln1_g, ln1_b, w_router, b_router,
           w_gu, b_gu, w_down, b_down, ln2_g, ln2_b):
    p = dict(w_in=w_in, conv_w=conv_w, conv_b=conv_b, w_rg=w_rg, b_rg=b_rg, w_ig=w_ig, b_ig=b_ig,
             lru_lambda=lru_lambda, q_norm_g=q_norm_g, w_uq=w_uq, kv_norm_g=kv_norm_g, w_uk=w_uk, w_uv=w_uv,
             w_o=w_o, ln1_g=ln1_g, ln1_b=ln1_b, w_router=w_router, b_router=b_router, w_gu=w_gu, b_gu=b_gu,
             w_down=w_down, b_down=b_down, ln2_g=ln2_g, ln2_b=ln2_b)
    return _trunk(x_prompt, x_sample, cache_ckv, cache_krope, state_conv, state_lru, p)
```
